```python
import jax
import jax.numpy as jnp
from jax import lax
import numpy as np

D_MODEL = 4096
BATCH = 32
SEQ = 256
DEPTH = 4
DEC_BATCH = 4
DEC_SEQ = 2048
PAST_LEN = 512

GRID_W = 64
HEAD_DIM = 128
A_HEADS = D_MODEL // HEAD_DIM // 2
A_KV_HEADS = 4
B_HEADS = D_MODEL // HEAD_DIM // 2
B_KV_HEADS = 4
WINDOW = 128
Q_BLOCK = 128
ROPE_BASE = 10000.0
ATTN_IN = (A_HEADS + B_HEADS + 2 * A_KV_HEADS + 2 * B_KV_HEADS) * HEAD_DIM
ATTN_MIX = (A_HEADS + B_HEADS) * HEAD_DIM
HGRN_HEADS = 32
HGRN_DK = 128
HGRN_DV = D_MODEL // HGRN_HEADS
HGRN_CHUNK = 16
HGRN_HK = HGRN_HEADS * HGRN_DK
HGRN_HV = HGRN_HEADS * HGRN_DV
HGRN_IN = 3 * HGRN_HK + 2 * HGRN_HV
N_EXPERTS = 32
TOP_K = 4
D_EXPERT = D_MODEL // 4
SWIGLU_ALPHA = 1.702
SWIGLU_LIMIT = 7.0
MOE_BLOCK = 128
N_ATTN_LAYERS = (DEPTH + 1) // 2
N_REC_LAYERS = DEPTH // 2
N_MOD = 6
EPS = 1e-6

kernel_name = 'hybrid_dit_ctxprefix_gqa_swa_hgrn2_moe_step'


def _rmsnorm(x, g):
    xf = x.astype(jnp.float32)
    y = xf * lax.rsqrt(jnp.mean(xf * xf, axis=-1, keepdims=True) + EPS)
    return (y * g.astype(jnp.float32)).astype(x.dtype)


def _modulation(cond, w, b):
    m = jax.nn.silu(cond) @ w + b
    return jnp.split(m[:, None, :], N_MOD, axis=-1)


def _modulate(x, shift, scale):
    return x * (1 + scale) + shift


def _axial_rope_tables(n_tokens):
    rows = n_tokens // GRID_W
    row = jnp.repeat(jnp.arange(rows), GRID_W).astype(jnp.float32)
    col = jnp.tile(jnp.arange(GRID_W), rows).astype(jnp.float32)
    n_freq = HEAD_DIM // 4
    inv_freq = ROPE_BASE ** (-jnp.arange(n_freq, dtype=jnp.float32) / n_freq)
    ang = jnp.concatenate([row[:, None] * inv_freq, col[:, None] * inv_freq], axis=-1)
    return jnp.cos(ang), jnp.sin(ang)


def _rope(x, cos, sin):
    xf = x.astype(jnp.float32).reshape(x.shape[:-1] + (HEAD_DIM // 2, 2))
    x1, x2 = xf[..., 0], xf[..., 1]
    c = cos[None, :, None, :]
    s = sin[None, :, None, :]
    out = jnp.stack([x1 * c - x2 * s, x1 * s + x2 * c], axis=-1).reshape(x.shape)
    return out.astype(x.dtype)


def _attend(q, k, v, valid, sink):
    s = jnp.einsum('bqgrd,bkgd->bgrqk', q, k).astype(jnp.float32) * (HEAD_DIM ** -0.5)
    if valid is not None:
        s = jnp.where(valid, s, -jnp.inf)
    if sink is not None:
        g, r = q.shape[2], q.shape[3]
        sk = jnp.broadcast_to(sink.astype(jnp.float32).reshape(g, r, 1, 1), s.shape[:-1] + (1,))
        p = jax.nn.softmax(jnp.concatenate([s, sk], axis=-1), axis=-1)[..., :-1]
    else:
        p = jax.nn.softmax(s, axis=-1)
    return jnp.einsum('bgrqk,bkgd->bqgrd', p.astype(v.dtype), v)


def _global_attention(q, k, v, sink):
    bsz, n, h, hd = q.shape
    g = k.shape[2]
    nb = n // Q_BLOCK
    qb = jnp.moveaxis(q.reshape(bsz, nb, Q_BLOCK, g, h // g, hd), 1, 0)
    out = lax.map(lambda qi: _attend(qi, k, v, None, sink), qb)
    return jnp.moveaxis(out, 0, 1).reshape(bsz, n, h, hd)


def _window_attention(q, k, v, k_ctx, v_ctx, sink):
    bsz, n, h, hd = q.shape
    g = k.shape[2]
    nb = n // Q_BLOCK
    span = Q_BLOCK + 2 * WINDOW
    n_ctx = k_ctx.shape[1]
    pad = ((0, 0), (WINDOW, WINDOW), (0, 0), (0, 0))
    k_pad = jnp.pad(k, pad)
    v_pad = jnp.pad(v, pad)
    qb = jnp.moveaxis(q.reshape(bsz, nb, Q_BLOCK, g, h // g, hd), 1, 0)
    q_off = jnp.arange(Q_BLOCK)
    k_off = jnp.arange(span) - WINDOW
    ctx_ok = jnp.ones((Q_BLOCK, n_ctx), bool)

    def block(args):
        qi, j = args
        start = j * Q_BLOCK
        kl = lax.dynamic_slice_in_dim(k_pad, start, span, axis=1)
        vl = lax.dynamic_slice_in_dim(v_pad, start, span, axis=1)
        qpos = start + q_off
        kpos = start + k_off
        local_ok = (jnp.abs(qpos[:, None] - kpos[None, :]) <= WINDOW) & (kpos[None, :] >= 0) & (kpos[None, :] < n)
        valid = jnp.concatenate([local_ok, ctx_ok], axis=1)
        return _attend(qi, jnp.concatenate([kl, k_ctx], axis=1), jnp.concatenate([vl, v_ctx], axis=1), valid, sink)

    out = lax.map(block, (qb, jnp.arange(nb)))
    return jnp.moveaxis(out, 0, 1).reshape(bsz, n, h, hd)


def _attn_qkv(h, w_in, qn_a, kn_a, qn_b, kn_b):
    bsz, n, _ = h.shape
    heads = (A_HEADS, A_KV_HEADS, A_KV_HEADS, B_HEADS, B_KV_HEADS, B_KV_HEADS)
    cuts = [HEAD_DIM * sum(heads[:j]) for j in range(1, len(heads))]
    parts = jnp.split(h @ w_in, cuts, axis=-1)
    qa, ka, va, qb, kb, vb = [t.reshape(bsz, n, nh, HEAD_DIM) for t, nh in zip(parts, heads)]
    return (_rmsnorm(qa, qn_a), _rmsnorm(ka, kn_a), va, _rmsnorm(qb, qn_b), _rmsnorm(kb, kn_b), vb)


def _attn_out(oa, ob, w_out):
    bsz, n = oa.shape[:2]
    return jnp.concatenate([oa, ob], axis=2).reshape(bsz, n, ATTN_MIX) @ w_out


def _chunk_scan(q, k, v, log_f, s0):
    bsz, n, h, _ = q.shape
    nc = n // HGRN_CHUNK
    causal = jnp.tril(jnp.ones((HGRN_CHUNK, HGRN_CHUNK), bool))[None, :, :, None, None]

    def to_chunks(t):
        return jnp.moveaxis(t.reshape(bsz, nc, HGRN_CHUNK, h, t.shape[-1]), 1, 0)

    def step(s, inp):
        qc, kc, vc, ac = inp
        b = jnp.cumsum(ac, axis=1)
        o_inter = jnp.einsum('blhk,bhkv->blhv', qc * jnp.exp(b), s)
        diff = b[:, :, None] - b[:, None, :]
        decay = jnp.exp(jnp.where(causal, diff, -jnp.inf))
        att = jnp.einsum('bthk,btshk,bshk->bhts', qc, decay, kc)
        o_intra = jnp.einsum('bhts,bshv->bthv', att, vc)
        b_last = b[:, -1]
        k_dec = kc * jnp.exp(b_last[:, None] - b)
        s_new = jnp.exp(b_last)[..., None] * s + jnp.einsum('bshk,bshv->bhkv', k_dec, vc)
        return s_new, o_inter + o_intra

    s_fin, o = lax.scan(step, s0, (to_chunks(q), to_chunks(k), to_chunks(v), to_chunks(log_f)))
    return jnp.moveaxis(o, 0, 1).reshape(bsz, n, h, v.shape[-1]), s_fin


def _hgrn_mixer(h, w_in, lb, norm_g, w_out, s0):
    bsz, n, _ = h.shape
    q, zf_f, zf_b, i, g = jnp.split(h @ w_in, [HGRN_HK, 2 * HGRN_HK, 3 * HGRN_HK, 3 * HGRN_HK + HGRN_HV], axis=-1)

    def heads(t, d):
        return t.reshape(bsz, n, HGRN_HEADS, d).astype(jnp.float32)

    def gates(zf, lbd):
        zf = heads(zf, HGRN_DK)
        lbd = lbd.reshape(HGRN_HEADS, HGRN_DK)
        log_f = jnp.logaddexp(jnp.log(lbd), jnp.log1p(-lbd) + jax.nn.log_sigmoid(zf))
        return log_f, (1 - lbd) * jax.nn.sigmoid(-zf)

    q = heads(q, HGRN_DK)
    v = heads(i, HGRN_DV)
    logf_f, k_f = gates(zf_f, lb[0])
    logf_b, k_b = gates(zf_b, lb[1])
    s0 = s0.astype(jnp.float32)
    flip = lambda t: jnp.flip(t, axis=1)
    o_f, s_f = _chunk_scan(q, k_f, v, logf_f, s0[:, 0])
    o_b, s_b = _chunk_scan(flip(q), flip(k_b), flip(v), flip(logf_b), s0[:, 1])
    o = o_f + flip(o_b)
    o = _rmsnorm(o, norm_g.reshape(HGRN_HEADS, HGRN_DV)) * jax.nn.silu(heads(g, HGRN_DV))
    out = o.reshape(bsz, n, HGRN_HV).astype(h.dtype) @ w_out
    return out, jnp.stack([s_f, s_b], axis=1).astype(h.dtype)


def _clamped_swiglu(hh):
    gate, up = jnp.split(hh, 2, axis=-1)
    gate = jnp.minimum(gate, SWIGLU_LIMIT)
    up = jnp.clip(up, -SWIGLU_LIMIT, SWIGLU_LIMIT)
    return gate * jax.nn.sigmoid(SWIGLU_ALPHA * gate) * (up + 1)


def _moe(x, w_router, b_router, w_gu, b_gu, w_down, b_down):
    bsz, n, d = x.shape
    xt = x.reshape(-1, d)
    t = xt.shape[0]
    logits = (xt @ w_router).astype(jnp.float32) + b_router.astype(jnp.float32)
    top_val, top_idx = lax.top_k(logits, TOP_K)
    gate = jax.nn.softmax(top_val, axis=-1).astype(x.dtype)
    n_assign = t * TOP_K
    flat_e = top_idx.reshape(-1)
    order = jnp.argsort(flat_e)
    e_sorted = flat_e[order]
    counts = jnp.bincount(flat_e, length=N_EXPERTS)
    padded = (counts + MOE_BLOCK - 1) // MOE_BLOCK * MOE_BLOCK
    pad_end = jnp.cumsum(padded)
    pad_start = pad_end - padded
    start = jnp.cumsum(counts) - counts
    dest = pad_start[e_sorted] + jnp.arange(n_assign, dtype=jnp.int32) - start[e_sorted]
    n_blocks = -(-(n_assign + N_EXPERTS * (MOE_BLOCK - 1)) // MOE_BLOCK)
    n_slots = n_blocks * MOE_BLOCK
    slot_tok = jnp.zeros((n_slots,), jnp.int32).at[dest].set((order // TOP_K).astype(jnp.int32))
    slot_w = jnp.zeros((n_slots,), x.dtype).at[dest].set(gate.reshape(-1)[order])
    block_e = jnp.minimum(jnp.searchsorted(pad_end, jnp.arange(n_blocks, dtype=jnp.int32) * MOE_BLOCK, side='right'), N_EXPERTS - 1)

    def expert_block(args):
        tok, w, e = args
        hh = xt[tok] @ w_gu[e] + b_gu[e]
        y = _clamped_swiglu(hh) @ w_down[e] + b_down[e]
        return y * w[:, None]

    ys = lax.map(expert_block, (slot_tok.reshape(n_blocks, MOE_BLOCK), slot_w.reshape(n_blocks, MOE_BLOCK), block_e))
    out = jax.ops.segment_sum(ys.reshape(n_slots, d), slot_tok, num_segments=t)
    return out.reshape(bsz, n, d)


def setup_inputs(seed: int = 0) -> dict:
    key = jax.random.key(seed)
    ks = jax.random.split(key, 30)
    f32 = jnp.float32

    def nrm(k, shape, std=1.0):
        return jax.random.normal(k, shape, f32) * std

    def gain(k, shape):
        return 1.0 + nrm(k, shape, 0.05)

    return {
        'x_prompt': nrm(ks[0], (BATCH, SEQ, D_MODEL)),
        'x_sample': nrm(ks[1], (DEC_BATCH, DEC_SEQ, D_MODEL)),
        'c': nrm(ks[2], (DEC_BATCH, D_MODEL)),
        'c_ctx': nrm(ks[3], (D_MODEL,)),
        'cache_k_a': nrm(ks[4], (DEC_BATCH, N_ATTN_LAYERS, PAST_LEN, A_KV_HEADS, HEAD_DIM)),
        'cache_v_a': nrm(ks[5], (DEC_BATCH, N_ATTN_LAYERS, PAST_LEN, A_KV_HEADS, HEAD_DIM)),
        'cache_k_b': nrm(ks[6], (DEC_BATCH, N_ATTN_LAYERS, PAST_LEN, B_KV_HEADS, HEAD_DIM)),
        'cache_v_b': nrm(ks[7], (DEC_BATCH, N_ATTN_LAYERS, PAST_LEN, B_KV_HEADS, HEAD_DIM)),
        'state_hgrn': nrm(ks[8], (DEC_BATCH, N_REC_LAYERS, 2, HGRN_HEADS, HGRN_DK, HGRN_DV)),
        'ada_w': nrm(ks[9], (DEPTH, D_MODEL, N_MOD * D_MODEL), 0.5 * D_MODEL ** -0.5),
        'ada_b': nrm(ks[10], (DEPTH, N_MOD * D_MODEL), 0.01),
        'norm1_g': gain(ks[11], (DEPTH, D_MODEL)),
        'norm2_g': gain(ks[12], (DEPTH, D_MODEL)),
        'attn_w_in': nrm(ks[13], (N_ATTN_LAYERS, D_MODEL, ATTN_IN), D_MODEL ** -0.5),
        'attn_qn_a': gain(ks[14], (N_ATTN_LAYERS, HEAD_DIM)),
        'attn_kn_a': gain(ks[15], (N_ATTN_LAYERS, HEAD_DIM)),
        'attn_qn_b': gain(ks[16], (N_ATTN_LAYERS, HEAD_DIM)),
        'attn_kn_b': gain(ks[17], (N_ATTN_LAYERS, HEAD_DIM)),
        'attn_sink_b': nrm(ks[18], (N_ATTN_LAYERS, B_HEADS)),
        'attn_w_out': nrm(ks[19], (N_ATTN_LAYERS, ATTN_MIX, D_MODEL), ATTN_MIX ** -0.5),
        'hgrn_w_in': nrm(ks[20], (N_REC_LAYERS, D_MODEL, HGRN_IN), D_MODEL ** -0.5),
        'hgrn_lb_logits': nrm(ks[21], (N_REC_LAYERS, 2, HGRN_HK)),
        'hgrn_norm_g': gain(ks[22], (N_REC_LAYERS, HGRN_HV)),
        'hgrn_w_out': nrm(ks[23], (N_REC_LAYERS, HGRN_HV, D_MODEL), HGRN_HV ** -0.5),
        'moe_w_router': nrm(ks[24], (DEPTH, D_MODEL, N_EXPERTS), D_MODEL ** -0.5),
        'moe_b_router': nrm(ks[25], (DEPTH, N_EXPERTS), 0.01),
        'moe_w_gu': nrm(ks[26], (DEPTH, N_EXPERTS, D_MODEL, 2 * D_EXPERT), D_MODEL ** -0.5),
        'moe_b_gu': nrm(ks[27], (DEPTH, N_EXPERTS, 2 * D_EXPERT), 0.01),
        'moe_w_down': nrm(ks[28], (DEPTH, N_EXPERTS, D_EXPERT, D_MODEL), D_EXPERT ** -0.5),
        'moe_b_down': nrm(ks[29], (DEPTH, N_EXPERTS, D_MODEL), 0.01),
    }


def reference(x_prompt, x_sample, c, c_ctx, cache_k_a, cache_v_a, cache_k_b, cache_v_b, state_hgrn,
              ada_w, ada_b, norm1_g, norm2_g, attn_w_in, attn_qn_a, attn_kn_a, attn_qn_b, attn_kn_b,
              attn_sink_b, attn_w_out, hgrn_w_in, hgrn_lb_logits, hgrn_norm_g, hgrn_w_out,
              moe_w_router, moe_b_router, moe_w_gu, moe_b_gu, moe_w_down, moe_b_down):
    xp, xs = x_prompt, x_sample
    cos, sin = _axial_rope_tables(xs.shape[1])
    p_lb = jax.nn.softmax(hgrn_lb_logits.astype(jnp.float32), axis=0)
    lower_bounds = jnp.cumsum(p_lb, axis=0) - p_lb[0]
    zero_state = jnp.zeros((xp.shape[0], 2, HGRN_HEADS, HGRN_DK, HGRN_DV), jnp.float32)
    new_ka, new_va, new_kb, new_vb, new_s = [], [], [], [], []
    for layer in range(DEPTH):
        p_sh1, p_sc1, p_g1, p_sh2, p_sc2, p_g2 = _modulation(c_ctx[None], ada_w[layer], ada_b[layer])
        s_sh1, s_sc1, s_g1, s_sh2, s_sc2, s_g2 = _modulation(c, ada_w[layer], ada_b[layer])
        hp = _modulate(_rmsnorm(xp, norm1_g[layer]), p_sh1, p_sc1)
        hs = _modulate(_rmsnorm(xs, norm1_g[layer]), s_sh1, s_sc1)
        li = layer // 2
        if layer % 2 == 0:
            qa, ka, va, qb, kb, vb = _attn_qkv(hp, attn_w_in[li], attn_qn_a[li], attn_kn_a[li], attn_qn_b[li], attn_kn_b[li])
            mp = _attn_out(_global_attention(qa, ka, va, None),
                           _global_attention(qb, kb, vb, attn_sink_b[li]), attn_w_out[li])
            new_ka.append(ka)
            new_va.append(va)
            new_kb.append(kb)
            new_vb.append(vb)
            qa, ka, va, qb, kb, vb = _attn_qkv(hs, attn_w_in[li], attn_qn_a[li], attn_kn_a[li], attn_qn_b[li], attn_kn_b[li])
            qa, ka, qb, kb = _rope(qa, cos, sin), _rope(ka, cos, sin), _rope(qb, cos, sin), _rope(kb, cos, sin)
            oa = _global_attention(qa, jnp.concatenate([ka, cache_k_a[:, li]], axis=1),
                                   jnp.concatenate([va, cache_v_a[:, li]], axis=1), None)
            ob = _window_attention(qb, kb, vb, cache_k_b[:, li], cache_v_b[:, li], attn_sink_b[li])
            ms = _attn_out(oa, ob, attn_w_out[li])
        else:
            mp, st = _hgrn_mixer(hp, hgrn_w_in[li], lower_bounds[li], hgrn_norm_g[li], hgrn_w_out[li], zero_state)
            new_s.append(st)
            ms, _ = _hgrn_mixer(hs, hgrn_w_in[li], lower_bounds[li], hgrn_norm_g[li], hgrn_w_out[li], state_hgrn[:, li])
        xp = xp + p_g1 * mp
        xs = xs + s_g1 * ms
        hp = _modulate(_rmsnorm(xp, norm2_g[layer]), p_sh2, p_sc2)
        hs = _modulate(_rmsnorm(xs, norm2_g[layer]), s_sh2, s_sc2)
        xp = xp + p_g2 * _moe(hp, moe_w_router[layer], moe_b_router[layer], moe_w_gu[layer], moe_b_gu[layer], moe_w_down[layer], moe_b_down[layer])
        xs = xs + s_g2 * _moe(hs, moe_w_router[layer], moe_b_router[layer], moe_w_gu[layer], moe_b_gu[layer], moe_w_down[layer], moe_b_down[layer])
    return (xp, xs, jnp.stack(new_ka, axis=1), jnp.stack(new_va, axis=1), jnp.stack(new_kb, axis=1), jnp.stack(new_vb, axis=1), jnp.stack(new_s, axis=1))
```

```python
import functools
import math

import jax
import jax.numpy as jnp
from jax import lax
from jax.experimental import pallas as pl
from jax.experimental.pallas import tpu as pltpu

F32 = jnp.float32
BF16 = jnp.bfloat16

HEAD_DIM = 128
A_HEADS = 16
A_KV_HEADS = 4
B_HEADS = 16
B_KV_HEADS = 4
WINDOW = 128
Q_BLOCK = 128
GRID_W = 64
ROPE_BASE = 10000.0
HGRN_HEADS = 32
HGRN_DK = 128
N_EXPERTS = 32
TOP_K = 4
SWIGLU_ALPHA = 1.702
SWIGLU_LIMIT = 7.0
N_MOD = 6
N_COND_ROWS = 8
EPS = 1e-6

VMEM_LIMIT = 48 * 1024 * 1024
HGRN_CHUNK = 128
MOE_BLOCK = 256


def _params(sem, vmem=VMEM_LIMIT):
    return pltpu.CompilerParams(dimension_semantics=sem, vmem_limit_bytes=vmem)


def _cond_row(i, tm, n_prompt_tok, dec_seq):
    tok0 = i * tm
    return jnp.where(tok0 < n_prompt_tok, 0, 1 + (tok0 - n_prompt_tok) // dec_seq)


def _ada_kernel(c_ref, w_ref, b_ref, o_ref):
    c = c_ref[...]
    s = c * jax.nn.sigmoid(c)
    o_ref[...] = jnp.dot(s.astype(BF16), w_ref[...].astype(BF16), preferred_element_type=F32) + b_ref[...]


def ada_modulation(cond, ada_w, ada_b, tn=512):
    depth, d, n = ada_w.shape
    return pl.pallas_call(
        _ada_kernel,
        out_shape=jax.ShapeDtypeStruct((depth, N_COND_ROWS, n), F32),
        grid=(depth, n // tn),
        in_specs=[
            pl.BlockSpec((N_COND_ROWS, d), lambda l, j: (0, 0)),
            pl.BlockSpec((None, d, tn), lambda l, j: (l, 0, j)),
            pl.BlockSpec((None, 1, tn), lambda l, j: (l, 0, j)),
        ],
        out_specs=pl.BlockSpec((None, N_COND_ROWS, tn), lambda l, j: (l, 0, j)),
        compiler_params=_params(("arbitrary", "arbitrary")),
        name="ada_modulation",
    )(cond, ada_w, ada_b.reshape(depth, 1, n))


def _norm_mod(x, g, sc, sh):
    ms = jnp.mean(x * x, axis=-1, keepdims=True)
    y = x * lax.rsqrt(ms + EPS) * g
    return y * (1.0 + sc) + sh


def _norm_mod_kernel(x_ref, g_ref, sh_ref, sc_ref, o_ref):
    o_ref[...] = _norm_mod(x_ref[...], g_ref[...], sc_ref[0], sh_ref[0]).astype(o_ref.dtype)


def _mod_spec(d, mod_base, slot, tm, layout):
    return pl.BlockSpec((1, 1, d), lambda i: (mod_base + _cond_row(i, tm, *layout) * N_MOD + slot, 0, 0))


def norm_modulate(x, gain, mods3, mod_base, layout, tm=256):
    t, d = x.shape
    return pl.pallas_call(
        _norm_mod_kernel,
        out_shape=jax.ShapeDtypeStruct((t, d), BF16),
        grid=(t // tm,),
        in_specs=[
            pl.BlockSpec((tm, d), lambda i: (i, 0)),
            pl.BlockSpec((1, d), lambda i: (0, 0)),
            _mod_spec(d, mod_base, 0, tm, layout),
            _mod_spec(d, mod_base, 1, tm, layout),
        ],
        out_specs=pl.BlockSpec((tm, d), lambda i: (i, 0)),
        compiler_params=_params(("arbitrary",)),
        name="norm_modulate",
    )(x, gain.reshape(1, d), mods3, mods3)


def _norm_router_kernel(x_ref, g_ref, sh_ref, sc_ref, wr_ref, br_ref, hp_ref, idx_ref, gate_ref):
    h = _norm_mod(x_ref[...], g_ref[...], sc_ref[0], sh_ref[0])
    half = h.shape[1] // 2
    bits = lax.bitcast_convert_type(h.astype(BF16).astype(F32), jnp.uint32)
    hp_ref[...] = (bits[:, half:] & jnp.uint32(0xFFFF0000)) | (bits[:, :half] >> 16)

    logits = jnp.dot(h, wr_ref[...], precision=lax.Precision.HIGHEST, preferred_element_type=F32) + br_ref[...]
    n_e = logits.shape[1]
    lane = lax.broadcasted_iota(jnp.int32, logits.shape, 1).astype(F32)
    vals = logits
    top_v, top_i = [], []
    for _ in range(TOP_K):
        m = jnp.max(vals, axis=-1, keepdims=True)
        first = jnp.min(jnp.where(vals == m, lane, float(n_e)), axis=-1, keepdims=True)
        vals = jnp.where(lane == first, -jnp.inf, vals)
        top_v.append(m)
        top_i.append(first)
    ex = [jnp.exp(v - top_v[0]) for v in top_v]
    denom = ex[0] + ex[1] + ex[2] + ex[3]
    for k in range(TOP_K):
        idx_ref[:, k:k + 1] = top_i[k].astype(jnp.int32)
        gate_ref[:, k:k + 1] = ex[k] / denom


def norm_modulate_route(x, gain, mods3, mod_base, w_router, b_router, layout, tm=256):
    t, d = x.shape
    n_e = w_router.shape[1]
    return pl.pallas_call(
        _norm_router_kernel,
        out_shape=(
            jax.ShapeDtypeStruct((t, d // 2), jnp.uint32),
            jax.ShapeDtypeStruct((t, TOP_K), jnp.int32),
            jax.ShapeDtypeStruct((t, TOP_K), F32),
        ),
        grid=(t // tm,),
        in_specs=[
            pl.BlockSpec((tm, d), lambda i: (i, 0)),
            pl.BlockSpec((1, d), lambda i: (0, 0)),
            _mod_spec(d, mod_base, 3, tm, layout),
            _mod_spec(d, mod_base, 4, tm, layout),
            pl.BlockSpec((d, n_e), lambda i: (0, 0)),
            pl.BlockSpec((1, n_e), lambda i: (0, 0)),
        ],
        out_specs=(
            pl.BlockSpec((tm, d // 2), lambda i: (i, 0)),
            pl.BlockSpec((tm, TOP_K), lambda i: (i, 0)),
            pl.BlockSpec((tm, TOP_K), lambda i: (i, 0)),
        ),
        compiler_params=_params(("arbitrary",)),
        name="norm_modulate_route",
    )(x, gain.reshape(1, d), mods3, mods3, w_router, b_router.reshape(1, n_e))


def _dense_kernel(a_ref, w_ref, o_ref, wbf_ref):
    @pl.when(pl.program_id(1) == 0)
    def _():
        wbf_ref[...] = w_ref[...].astype(BF16)

    o_ref[...] = jnp.dot(a_ref[...], wbf_ref[...], preferred_element_type=F32).astype(o_ref.dtype)


def _dense_residual_kernel(a_ref, w_ref, x_ref, gate_ref, o_ref, wbf_ref):
    @pl.when(pl.program_id(1) == 0)
    def _():
        wbf_ref[...] = w_ref[...].astype(BF16)

    o_ref[...] = x_ref[...] + gate_ref[0] * jnp.dot(a_ref[...], wbf_ref[...], preferred_element_type=F32)


def dense(a, w3, layer, out_dtype, tm=512, tn=512):
    t, k = a.shape
    n = w3.shape[2]
    return pl.pallas_call(
        _dense_kernel,
        out_shape=jax.ShapeDtypeStruct((t, n), out_dtype),
        grid=(n // tn, t // tm),
        in_specs=[
            pl.BlockSpec((tm, k), lambda j, i: (i, 0)),
            pl.BlockSpec((None, k, tn), lambda j, i: (layer, 0, j)),
        ],
        out_specs=pl.BlockSpec((tm, tn), lambda j, i: (i, j)),
        scratch_shapes=[pltpu.VMEM((k, tn), BF16)],
        compiler_params=_params(("arbitrary", "arbitrary")),
        name="dense",
    )(a, w3)


def dense_residual(a, w3, layer, x, mods3, mod_base, slot, layout, tm=512, tn=512):
    t, k = a.shape
    n = w3.shape[2]
    return pl.pallas_call(
        _dense_residual_kernel,
        out_shape=jax.ShapeDtypeStruct((t, n), F32),
        grid=(n // tn, t // tm),
        in_specs=[
            pl.BlockSpec((tm, k), lambda j, i: (i, 0)),
            pl.BlockSpec((None, k, tn), lambda j, i: (layer, 0, j)),
            pl.BlockSpec((tm, tn), lambda j, i: (i, j)),
            pl.BlockSpec((1, 1, tn), lambda j, i: (mod_base + _cond_row(i, tm, *layout) * N_MOD + slot, 0, j)),
        ],
        out_specs=pl.BlockSpec((tm, tn), lambda j, i: (i, j)),
        scratch_shapes=[pltpu.VMEM((k, tn), BF16)],
        input_output_aliases={2: 0},
        compiler_params=_params(("arbitrary", "arbitrary")),
        name="dense_residual",
    )(a, w3, x, mods3)


def _head_rmsnorm(x, g):
    return x * lax.rsqrt(jnp.mean(x * x, axis=-1, keepdims=True) + EPS) * g


def _qkv_post_kernel(qkv_ref, cos_ref, sin_ref, qna_ref, kna_ref, qnb_ref, knb_ref, o_ref, kn_ref):
    cos = cos_ref[...]
    sin = sin_ref[...]
    even = (lax.broadcasted_iota(jnp.int32, cos.shape, 1) & 1) == 0
    segs = ((A_HEADS, qna_ref, False), (A_KV_HEADS, kna_ref, True), (A_KV_HEADS, None, False),
            (B_HEADS, qnb_ref, False), (B_KV_HEADS, knb_ref, True), (B_KV_HEADS, None, False))
    head = 0
    k_head = 0
    for n_heads, gain_ref, is_key in segs:
        for _ in range(n_heads):
            sl = slice(head * HEAD_DIM, (head + 1) * HEAD_DIM)
            x = qkv_ref[:, sl]
            if gain_ref is not None:
                x = _head_rmsnorm(x, gain_ref[...])
                if is_key:
                    kn_ref[:, k_head * HEAD_DIM:(k_head + 1) * HEAD_DIM] = x
                    k_head += 1
                partner = jnp.where(even, pltpu.roll(x, HEAD_DIM - 1, axis=1), pltpu.roll(x, 1, axis=1))
                x = x * cos + partner * sin
            o_ref[:, sl] = x.astype(o_ref.dtype)
            head += 1


def qkv_post(qkv, cos, sin, qn_a, kn_a, qn_b, kn_b, tm=256):
    t, n = qkv.shape
    n_k = (A_KV_HEADS + B_KV_HEADS) * HEAD_DIM
    gain = lambda g: g.reshape(1, HEAD_DIM)
    gspec = pl.BlockSpec((1, HEAD_DIM), lambda i: (0, 0))
    return pl.pallas_call(
        _qkv_post_kernel,
        out_shape=(jax.ShapeDtypeStruct((t, n), BF16), jax.ShapeDtypeStruct((t, n_k), F32)),
        grid=(t // tm,),
        in_specs=[
            pl.BlockSpec((tm, n), lambda i: (i, 0)),
            pl.BlockSpec((tm, HEAD_DIM), lambda i: (i, 0)),
            pl.BlockSpec((tm, HEAD_DIM), lambda i: (i, 0)),
            gspec, gspec, gspec, gspec,
        ],
        out_specs=(pl.BlockSpec((tm, n), lambda i: (i, 0)), pl.BlockSpec((tm, n_k), lambda i: (i, 0))),
        compiler_params=_params(("arbitrary",)),
        name="qkv_post",
    )(qkv, cos, sin, gain(qn_a), gain(kn_a), gain(qn_b), gain(kn_b))


def _attn_kernel(*refs, rep, has_cache, window, has_sink):
    it = iter(refs)
    q_ref, k_ref, v_ref = next(it), next(it), next(it)
    kc_ref, vc_ref = (next(it), next(it)) if has_cache else (None, None)
    sink_ref = next(it) if has_sink else None
    o_ref = next(it)

    qi = pl.program_id(2)
    q = q_ref[...]
    qs = jnp.concatenate([q[:, r * HEAD_DIM:(r + 1) * HEAD_DIM] for r in range(rep)], axis=0)
    nt = (((1,), (1,)), ((), ()))
    scale = HEAD_DIM ** -0.5
    if window:
        span = Q_BLOCK + 2 * WINDOW
        n_local = k_ref.shape[0]
        start = pl.multiple_of(jnp.clip(qi * Q_BLOCK - WINDOW, 0, n_local - span), Q_BLOCK)
        k1 = k_ref[pl.ds(start, span), :]
        v1 = v_ref[pl.ds(start, span), :]
    else:
        k1 = k_ref[...]
        v1 = v_ref[...]
    s1 = lax.dot_general(qs, k1, nt, preferred_element_type=F32) * scale
    if window:
        qpos = qi * Q_BLOCK + (lax.broadcasted_iota(jnp.int32, s1.shape, 0) & (Q_BLOCK - 1))
        kpos = start + lax.broadcasted_iota(jnp.int32, s1.shape, 1)
        s1 = jnp.where(jnp.abs(qpos - kpos) <= WINDOW, s1, -jnp.inf)
    m = jnp.max(s1, axis=-1, keepdims=True)
    if has_cache:
        s2 = lax.dot_general(qs, kc_ref[...], nt, preferred_element_type=F32) * scale
        m = jnp.maximum(m, jnp.max(s2, axis=-1, keepdims=True))
    if has_sink:
        sink = sink_ref[...]
        m = jnp.maximum(m, sink)
    p1 = jnp.exp(s1 - m)
    denom = jnp.sum(p1, axis=-1, keepdims=True)
    acc = jnp.dot(p1.astype(BF16), v1, preferred_element_type=F32)
    if has_cache:
        p2 = jnp.exp(s2 - m)
        denom = denom + jnp.sum(p2, axis=-1, keepdims=True)
        acc = acc + jnp.dot(p2.astype(BF16), vc_ref[...], preferred_element_type=F32)
    if has_sink:
        denom = denom + jnp.exp(sink - m)
    o = acc / denom
    o_ref[...] = jnp.concatenate([o[r * Q_BLOCK:(r + 1) * Q_BLOCK] for r in range(rep)], axis=1).astype(o_ref.dtype)


def attention(qkv, q_col, k_col, v_col, n_heads, n_kv, row0, n_seq, seq_len, cache_k=None, cache_v=None,
              sink=None, window=False):
    rep = n_heads // n_kv
    qw = rep * HEAD_DIM
    nqb = seq_len // Q_BLOCK
    rb0 = row0 // Q_BLOCK
    sb0 = row0 // seq_len
    in_specs = [
        pl.BlockSpec((Q_BLOCK, qw), lambda b, g, i: (rb0 + b * nqb + i, q_col // qw + g)),
        pl.BlockSpec((seq_len, HEAD_DIM), lambda b, g, i: (sb0 + b, k_col // HEAD_DIM + g)),
        pl.BlockSpec((seq_len, HEAD_DIM), lambda b, g, i: (sb0 + b, v_col // HEAD_DIM + g)),
    ]
    args = [qkv, qkv, qkv]
    if cache_k is not None:
        n_past = cache_k.shape[1]
        spec = pl.BlockSpec((None, n_past, HEAD_DIM), lambda b, g, i: (b, 0, g))
        in_specs += [spec, spec]
        args += [cache_k, cache_v]
    if sink is not None:
        sink_rows = jnp.repeat(sink.astype(F32), Q_BLOCK).reshape(n_kv * rep * Q_BLOCK, 1)
        in_specs.append(pl.BlockSpec((rep * Q_BLOCK, 1), lambda b, g, i: (g, 0)))
        args.append(sink_rows)
    return pl.pallas_call(
        functools.partial(_attn_kernel, rep=rep, has_cache=cache_k is not None, window=window,
                          has_sink=sink is not None),
        out_shape=jax.ShapeDtypeStruct((n_seq * seq_len, n_heads * HEAD_DIM), BF16),
        grid=(n_seq, n_kv, nqb),
        in_specs=in_specs,
        out_specs=pl.BlockSpec((Q_BLOCK, qw), lambda b, g, i: (b * nqb + i, g)),
        compiler_params=_params(("arbitrary", "arbitrary", "arbitrary")),
        name="attention",
    )(*args)


def _hgrn_chunk(q, z, v, lb, st, row, rowc, colc, forward):
    c = q.shape[0]
    e = jnp.exp(-jnp.abs(z))
    r = 1.0 / (1.0 + e)
    pos = z >= 0
    sig_pos = jnp.where(pos, r, e * r)
    sig_neg = jnp.where(pos, e * r, r)
    f = lb + (1.0 - lb) * sig_pos
    logf = jnp.log(f)
    k = (1.0 - lb) * sig_neg
    nt = (((1,), (1,)), ((), ()))

    a = logf
    tot = logf
    att = jnp.zeros((c, c), F32)
    half = 1
    while half < c:
        up = (row & half) != 0
        if forward:
            arg = jnp.where(up, a, tot - a)
        else:
            arg = jnp.where(up, a - logf, tot - a + logf)
        ex = jnp.exp(arg)
        q_role = up if forward else jnp.logical_not(up)
        qt = jnp.where(q_role, q * ex, 0.0).astype(BF16)
        kt = jnp.where(q_role, 0.0, k * ex).astype(BF16)
        lvl = lax.dot_general(qt, kt, nt, preferred_element_type=F32)
        if 2 * half < c:
            shift = int(math.log2(2 * half))
            lvl = jnp.where((rowc >> shift) == (colc >> shift), lvl, 0.0)
        att = att + lvl
        tot_dn = pltpu.roll(tot, half, axis=0)
        tot_up = pltpu.roll(tot, c - half, axis=0)
        a = a + jnp.where(up, tot_dn, 0.0)
        tot = tot + jnp.where(up, tot_dn, tot_up)
        half *= 2

    if forward:
        q_dec = q * jnp.exp(a)
        k_dec = k * jnp.exp(tot - a)
    else:
        q_dec = q * jnp.exp(tot - a + logf)
        k_dec = k * jnp.exp(a - logf)
    vb = v.astype(BF16)
    o = lax.dot_general(q_dec.astype(BF16), st.astype(BF16), nt, preferred_element_type=F32)
    o = o + jnp.dot(att.astype(BF16), vb, preferred_element_type=F32)
    o = o + jnp.sum(q * k, axis=-1, keepdims=True) * v
    upd = lax.dot_general(vb, k_dec.astype(BF16), (((0,), (0,)), ((), ())), preferred_element_type=F32)
    st_new = jnp.exp(tot[0:1, :]) * st + upd
    return o, st_new


def _hgrn_scan_kernel(*refs, hpg, has_s0, emit_state):
    it = iter(refs)
    qf, zf, vf, qb, zb, vb, lbf, lbb = [next(it) for _ in range(8)]
    s0 = next(it) if has_s0 else None
    of, ob = next(it), next(it)
    sfin = next(it) if emit_state else None
    st = next(it)
    step = pl.program_id(2)
    c = qf.shape[0]

    @pl.when(step == 0)
    def _init():
        if has_s0:
            for d in range(2):
                for h in range(hpg):
                    st[d, h] = s0[0, d, h].T
        else:
            st[...] = jnp.zeros(st.shape, F32)

    row = lax.broadcasted_iota(jnp.int32, (c, HGRN_DK), 0)
    rowc = lax.broadcasted_iota(jnp.int32, (c, c), 0)
    colc = lax.broadcasted_iota(jnp.int32, (c, c), 1)
    for h in range(hpg):
        sl = slice(h * HGRN_DK, (h + 1) * HGRN_DK)
        for d, (q_r, z_r, v_r, lb_r, o_r) in enumerate(((qf, zf, vf, lbf, of), (qb, zb, vb, lbb, ob))):
            o, st_new = _hgrn_chunk(q_r[:, sl], z_r[:, sl], v_r[:, sl], lb_r[:, sl], st[d, h], row, rowc, colc,
                                    forward=(d == 0))
            o_r[:, sl] = o
            st[d, h] = st_new

    if emit_state:
        @pl.when(step == pl.num_programs(2) - 1)
        def _fin():
            for d in range(2):
                for h in range(hpg):
                    sfin[0, d, h] = st[d, h].T


def hgrn_scan(proj, lb, row0, n_seq, seq_len, s0=None, emit_state=False, chunk=HGRN_CHUNK, hpg=4):
    hk = HGRN_HEADS * HGRN_DK
    hw = hpg * HGRN_DK
    n_hg = HGRN_HEADS // hpg
    nc = seq_len // chunk
    rb0 = row0 // chunk
    seg = hk // hw

    def fwd(col):
        return pl.BlockSpec((chunk, hw), lambda b, g, c: (rb0 + b * nc + c, col * seg + g))

    def bwd(col):
        return pl.BlockSpec((chunk, hw), lambda b, g, c: (rb0 + b * nc + nc - 1 - c, col * seg + g))

    in_specs = [fwd(0), fwd(1), fwd(3), bwd(0), bwd(2), bwd(3),
                pl.BlockSpec((1, hw), lambda b, g, c: (0, g)), pl.BlockSpec((1, hw), lambda b, g, c: (0, g))]
    args = [proj, proj, proj, proj, proj, proj, lb[0:1], lb[1:2]]
    state_spec = pl.BlockSpec((1, 2, hpg, HGRN_DK, HGRN_DK), lambda b, g, c: (b, 0, g, 0, 0))
    if s0 is not None:
        in_specs.append(state_spec)
        args.append(s0)
    n_rows = n_seq * seq_len
    out_shape = [jax.ShapeDtypeStruct((n_rows, hk), F32), jax.ShapeDtypeStruct((n_rows, hk), F32)]
    out_specs = [pl.BlockSpec((chunk, hw), lambda b, g, c: (b * nc + c, g)),
                 pl.BlockSpec((chunk, hw), lambda b, g, c: (b * nc + nc - 1 - c, g))]
    if emit_state:
        out_shape.append(jax.ShapeDtypeStruct((n_seq, 2, HGRN_HEADS, HGRN_DK, HGRN_DK), F32))
        out_specs.append(state_spec)
    return pl.pallas_call(
        functools.partial(_hgrn_scan_kernel, hpg=hpg, has_s0=s0 is not None, emit_state=emit_state),
        out_shape=tuple(out_shape),
        grid=(n_seq, n_hg, nc),
        in_specs=in_specs,
        out_specs=tuple(out_specs),
        scratch_shapes=[pltpu.VMEM((2, hpg, HGRN_DK, HGRN_DK), F32)],
        compiler_params=_params(("arbitrary", "arbitrary", "arbitrary")),
        name="hgrn_scan",
    )(*args)


def _hgrn_post_kernel(of_ref, ob_ref, g_ref, ng_ref, o_ref):
    for h in range(HGRN_HEADS):
        sl = slice(h * HGRN_DK, (h + 1) * HGRN_DK)
        o = of_ref[:, sl] + ob_ref[:, sl]
        g = g_ref[:, sl]
        y = _head_rmsnorm(o, ng_ref[:, sl]) * (g * jax.nn.sigmoid(g))
        o_ref[:, sl] = y.astype(o_ref.dtype)


def hgrn_post(o_f, o_b, proj, row0, norm_g, tm=256):
    t, hk = o_f.shape
    rb0 = row0 // tm
    spec = pl.BlockSpec((tm, hk), lambda i: (i, 0))
    return pl.pallas_call(
        _hgrn_post_kernel,
        out_shape=jax.ShapeDtypeStruct((t, hk), BF16),
        grid=(t // tm,),
        in_specs=[spec, spec, pl.BlockSpec((tm, hk), lambda i: (rb0 + i, 4)), pl.BlockSpec((1, hk), lambda i: (0, 0))],
        out_specs=spec,
        compiler_params=_params(("arbitrary",)),
        name="hgrn_post",
    )(o_f, o_b, proj, norm_g.reshape(1, hk))


def moe_layout(idx, block):
    n_assign = idx.size
    flat_e = idx.reshape(-1)
    order = jnp.argsort(flat_e).astype(jnp.int32)
    e_sorted = flat_e[order]
    counts = jnp.sum((flat_e[:, None] == jnp.arange(N_EXPERTS, dtype=jnp.int32)[None, :]).astype(jnp.int32), axis=0)
    padded = (counts + block - 1) // block * block
    pad_end = jnp.cumsum(padded)
    pad_start = pad_end - padded
    start = jnp.cumsum(counts) - counts
    n_blocks = -(-(n_assign + N_EXPERTS * (block - 1)) // block)
    block_e = jnp.minimum(
        jnp.searchsorted(pad_end, jnp.arange(n_blocks, dtype=jnp.int32) * block, side="right"), N_EXPERTS - 1
    ).astype(jnp.int32)
    slot = jnp.arange(n_blocks * block, dtype=jnp.int32)
    slot_e = jnp.repeat(block_e, block)
    off = slot - pad_start[slot_e]
    valid = (off >= 0) & (off < counts[slot_e])
    src = jnp.clip(start[slot_e] + off, 0, n_assign - 1)
    slot_tok = jnp.where(valid, order[src] // TOP_K, 0).astype(jnp.int32)
    dest = pad_start[e_sorted] + jnp.arange(n_assign, dtype=jnp.int32) - start[e_sorted]
    slot_of = jnp.zeros((n_assign,), jnp.int32).at[order].set(dest.astype(jnp.int32), unique_indices=True)
    return slot_tok, slot_of, block_e


def _gather_kernel(tok_ref, hp_hbm, o_ref, buf, sem):
    rows, half = buf.shape
    base = pl.program_id(0) * rows

    def issue(r, carry):
        pltpu.make_async_copy(hp_hbm.at[tok_ref[base + r]], buf.at[r], sem).start()
        return carry

    lax.fori_loop(0, rows, issue, 0, unroll=8)
    pltpu.make_async_copy(hp_hbm.at[pl.ds(0, rows)], buf, sem).wait()
    w = buf[...]
    o_ref[:, :half] = lax.bitcast_convert_type(w << 16, F32).astype(BF16)
    o_ref[:, half:] = lax.bitcast_convert_type(w & jnp.uint32(0xFFFF0000), F32).astype(BF16)


def moe_gather(slot_tok, hp, rows=256):
    n_slots = slot_tok.shape[0]
    half = hp.shape[1]
    return pl.pallas_call(
        _gather_kernel,
        out_shape=jax.ShapeDtypeStruct((n_slots, 2 * half), BF16),
        grid_spec=pltpu.PrefetchScalarGridSpec(
            num_scalar_prefetch=1,
            grid=(n_slots // rows,),
            in_specs=[pl.BlockSpec(memory_space=pl.ANY)],
            out_specs=pl.BlockSpec((rows, 2 * half), lambda i, tok: (i, 0)),
            scratch_shapes=[pltpu.VMEM((rows, half), jnp.uint32), pltpu.SemaphoreType.DMA],
        ),
        compiler_params=_params(("arbitrary",)),
        name="moe_gather",
    )(slot_tok, hp)


def _first_block_of_expert(be_ref, b):
    return jnp.logical_or(b == 0, be_ref[b] != be_ref[jnp.maximum(b - 1, 0)])


def _moe_gu_kernel(be_ref, x_ref, wg_ref, wu_ref, bg_ref, bu_ref, o_ref, wg_bf, wu_bf):
    @pl.when(_first_block_of_expert(be_ref, pl.program_id(1)))
    def _():
        wg_bf[...] = wg_ref[...].astype(BF16)
        wu_bf[...] = wu_ref[...].astype(BF16)

    x = x_ref[...]
    gate = jnp.dot(x, wg_bf[...], preferred_element_type=F32) + bg_ref[...]
    up = jnp.dot(x, wu_bf[...], preferred_element_type=F32) + bu_ref[...]
    gate = jnp.minimum(gate, SWIGLU_LIMIT)
    up = jnp.clip(up, -SWIGLU_LIMIT, SWIGLU_LIMIT)
    o_ref[...] = (gate * jax.nn.sigmoid(SWIGLU_ALPHA * gate) * (up + 1.0)).astype(o_ref.dtype)


def moe_gate_up(block_e, x_sorted, w_gu, b_gu, layer, block, tj=256):
    n_slots, d = x_sorted.shape
    de = w_gu.shape[3] // 2
    n_j = de // tj
    b4 = b_gu.reshape(b_gu.shape[0], N_EXPERTS, 1, 2 * de)
    return pl.pallas_call(
        _moe_gu_kernel,
        out_shape=jax.ShapeDtypeStruct((n_slots, de), BF16),
        grid_spec=pltpu.PrefetchScalarGridSpec(
            num_scalar_prefetch=1,
            grid=(n_j, n_slots // block),
            in_specs=[
                pl.BlockSpec((block, d), lambda j, b, be: (b, 0)),
                pl.BlockSpec((None, None, d, tj), lambda j, b, be: (layer, be[b], 0, j)),
                pl.BlockSpec((None, None, d, tj), lambda j, b, be: (layer, be[b], 0, n_j + j)),
                pl.BlockSpec((None, None, 1, tj), lambda j, b, be: (layer, be[b], 0, j)),
                pl.BlockSpec((None, None, 1, tj), lambda j, b, be: (layer, be[b], 0, n_j + j)),
            ],
            out_specs=pl.BlockSpec((block, tj), lambda j, b, be: (b, j)),
            scratch_shapes=[pltpu.VMEM((d, tj), BF16), pltpu.VMEM((d, tj), BF16)],
        ),
        compiler_params=_params(("arbitrary", "arbitrary")),
        name="moe_gate_up",
    )(block_e, x_sorted, w_gu, w_gu, b4, b4)


def _moe_down_kernel(be_ref, a_ref, w_ref, b_ref, o_ref, w_bf):
    @pl.when(_first_block_of_expert(be_ref, pl.program_id(1)))
    def _():
        w_bf[...] = w_ref[...].astype(BF16)

    o_ref[...] = jnp.dot(a_ref[...], w_bf[...], preferred_element_type=F32) + b_ref[...]


def moe_down(block_e, act, w_down, b_down, layer, block, tn=1024):
    n_slots, de = act.shape
    d = w_down.shape[3]
    b4 = b_down.reshape(b_down.shape[0], N_EXPERTS, 1, d)
    return pl.pallas_call(
        _moe_down_kernel,
        out_shape=jax.ShapeDtypeStruct((n_slots, d), F32),
        grid_spec=pltpu.PrefetchScalarGridSpec(
            num_scalar_prefetch=1,
            grid=(d // tn, n_slots // block),
            in_specs=[
                pl.BlockSpec((block, de), lambda n, b, be: (b, 0)),
                pl.BlockSpec((None, None, de, tn), lambda n, b, be: (layer, be[b], 0, n)),
                pl.BlockSpec((None, None, 1, tn), lambda n, b, be: (layer, be[b], 0, n)),
            ],
            out_specs=pl.BlockSpec((block, tn), lambda n, b, be: (b, n)),
            scratch_shapes=[pltpu.VMEM((de, tn), BF16)],
        ),
        compiler_params=_params(("arbitrary", "arbitrary")),
        name="moe_down",
    )(block_e, act, w_down, b4)


def _combine_kernel(slot_ref, y_hbm, x_ref, gate_ref, g2_ref, o_ref, buf, sem):
    tb = buf.shape[1]
    base = pl.program_id(0) * tb * TOP_K

    def issue(t, carry):
        for k in range(TOP_K):
            pltpu.make_async_copy(y_hbm.at[slot_ref[base + t * TOP_K + k]], buf.at[k, t], sem).start()
        return carry

    lax.fori_loop(0, tb, issue, 0, unroll=2)
    for k in range(TOP_K):
        pltpu.make_async_copy(y_hbm.at[pl.ds(0, tb)], buf.at[k], sem).wait()
    gate = gate_ref[...]
    acc = gate[:, 0:1] * buf[0]
    for k in range(1, TOP_K):
        acc = acc + gate[:, k:k + 1] * buf[k]
    o_ref[...] = x_ref[...] + g2_ref[0] * acc


def moe_combine(slot_of, y, x, gate, mods3, mod_base, layout, tb=64):
    t, d = x.shape
    return pl.pallas_call(
        _combine_kernel,
        out_shape=jax.ShapeDtypeStruct((t, d), F32),
        grid_spec=pltpu.PrefetchScalarGridSpec(
            num_scalar_prefetch=1,
            grid=(t // tb,),
            in_specs=[
                pl.BlockSpec(memory_space=pl.ANY),
                pl.BlockSpec((tb, d), lambda i, s: (i, 0)),
                pl.BlockSpec((tb, TOP_K), lambda i, s: (i, 0)),
                pl.BlockSpec((1, 1, d), lambda i, s: (mod_base + _cond_row(i, tb, *layout) * N_MOD + 5, 0, 0)),
            ],
            out_specs=pl.BlockSpec((tb, d), lambda i, s: (i, 0)),
            scratch_shapes=[pltpu.VMEM((TOP_K, tb, d), F32), pltpu.SemaphoreType.DMA],
        ),
        input_output_aliases={2: 0},
        compiler_params=_params(("arbitrary",)),
        name="moe_combine",
    )(slot_of, y, x, gate, mods3)


def moe(x, gain, mods3, mod_base, layer, w_router, b_router, w_gu, b_gu, w_down, b_down, layout):
    hp, idx, gate = norm_modulate_route(x, gain, mods3, mod_base, w_router, b_router, layout)
    slot_tok, slot_of, block_e = moe_layout(idx, MOE_BLOCK)
    x_sorted = moe_gather(slot_tok, hp, rows=MOE_BLOCK)
    act = moe_gate_up(block_e, x_sorted, w_gu, b_gu, layer, MOE_BLOCK)
    y = moe_down(block_e, act, w_down, b_down, layer, MOE_BLOCK)
    return moe_combine(slot_of, y, x, gate, mods3, mod_base, layout)


def _rope_tables(n_prompt_tok, dec_batch, dec_seq):
    pos = jnp.arange(dec_seq)
    row = (pos // GRID_W).astype(F32)
    col = (pos % GRID_W).astype(F32)
    n_freq = HEAD_DIM // 4
    inv_freq = ROPE_BASE ** (-jnp.arange(n_freq, dtype=F32) / n_freq)
    ang = jnp.concatenate([row[:, None] * inv_freq, col[:, None] * inv_freq], axis=-1)
    cos = jnp.repeat(jnp.cos(ang), 2, axis=-1)
    sin = jnp.repeat(jnp.sin(ang), 2, axis=-1) * jnp.tile(jnp.array([-1.0, 1.0], F32), HEAD_DIM // 2)
    cos = jnp.concatenate([jnp.ones((n_prompt_tok, HEAD_DIM), F32), jnp.tile(cos, (dec_batch, 1))], axis=0)
    sin = jnp.concatenate([jnp.zeros((n_prompt_tok, HEAD_DIM), F32), jnp.tile(sin, (dec_batch, 1))], axis=0)
    return cos, sin


def kernel(x_prompt, x_sample, c, c_ctx, cache_k_a, cache_v_a, cache_k_b, cache_v_b, state_hgrn,
           ada_w, ada_b, norm1_g, norm2_g, attn_w_in, attn_qn_a, attn_kn_a, attn_qn_b, attn_kn_b,
           attn_sink_b, attn_w_out, hgrn_w_in, hgrn_lb_logits, hgrn_norm_g, hgrn_w_out,
           moe_w_router, moe_b_router, moe_w_gu, moe_b_gu, moe_w_down, moe_b_down):
    batch, seq, d = x_prompt.shape
    dec_batch, dec_seq, _ = x_sample.shape
    depth = ada_w.shape[0]
    n_p = batch * seq
    n_s = dec_batch * dec_seq
    layout = (n_p, dec_seq)
    assert 1 + dec_batch <= N_COND_ROWS

    cond = jnp.zeros((N_COND_ROWS, d), F32).at[0].set(c_ctx).at[1:1 + dec_batch].set(c)
    mods3 = ada_modulation(cond, ada_w, ada_b).reshape(depth * N_COND_ROWS * N_MOD, 1, d)
    x = jnp.concatenate([x_prompt.reshape(n_p, d), x_sample.reshape(n_s, d)], axis=0)
    cos, sin = _rope_tables(n_p, dec_batch, dec_seq)
    p_lb = jax.nn.softmax(hgrn_lb_logits.astype(F32), axis=0)
    lower_bounds = jnp.cumsum(p_lb, axis=0) - p_lb[0]

    qa_col = 0
    ka_col = qa_col + A_HEADS * HEAD_DIM
    va_col = ka_col + A_KV_HEADS * HEAD_DIM
    qb_col = va_col + A_KV_HEADS * HEAD_DIM
    kb_col = qb_col + B_HEADS * HEAD_DIM
    vb_col = kb_col + B_KV_HEADS * HEAD_DIM
    kv_a = A_KV_HEADS * HEAD_DIM
    kv_b = B_KV_HEADS * HEAD_DIM

    new_ka, new_va, new_kb, new_vb, new_s = [], [], [], [], []
    for layer in range(depth):
        mod_base = layer * N_COND_ROWS * N_MOD
        li = layer // 2
        h = norm_modulate(x, norm1_g[layer], mods3, mod_base, layout)
        if layer % 2 == 0:
            qkv = dense(h, attn_w_in, li, F32)
            qkv_bf, kn = qkv_post(qkv, cos, sin, attn_qn_a[li], attn_kn_a[li], attn_qn_b[li], attn_kn_b[li])
            new_ka.append(kn[:n_p, :kv_a].reshape(batch, seq, A_KV_HEADS, HEAD_DIM))
            new_kb.append(kn[:n_p, kv_a:].reshape(batch, seq, B_KV_HEADS, HEAD_DIM))
            new_va.append(qkv[:n_p, va_col:va_col + kv_a].reshape(batch, seq, A_KV_HEADS, HEAD_DIM))
            new_vb.append(qkv[:n_p, vb_col:vb_col + kv_b].reshape(batch, seq, B_KV_HEADS, HEAD_DIM))
            n_past = cache_k_a.shape[2]
            cache = lambda t, nkv: t[:, li].reshape(dec_batch, n_past, nkv * HEAD_DIM).astype(BF16)
            sink = attn_sink_b[li]
            o_pa = attention(qkv_bf, qa_col, ka_col, va_col, A_HEADS, A_KV_HEADS, 0, batch, seq)
            o_pb = attention(qkv_bf, qb_col, kb_col, vb_col, B_HEADS, B_KV_HEADS, 0, batch, seq, sink=sink)
            o_sa = attention(qkv_bf, qa_col, ka_col, va_col, A_HEADS, A_KV_HEADS, n_p, dec_batch, dec_seq,
                             cache_k=cache(cache_k_a, A_KV_HEADS), cache_v=cache(cache_v_a, A_KV_HEADS))
            o_sb = attention(qkv_bf, qb_col, kb_col, vb_col, B_HEADS, B_KV_HEADS, n_p, dec_batch, dec_seq,
                             cache_k=cache(cache_k_b, B_KV_HEADS), cache_v=cache(cache_v_b, B_KV_HEADS),
                             sink=sink, window=True)
            mix = jnp.concatenate([jnp.concatenate([o_pa, o_pb], axis=1), jnp.concatenate([o_sa, o_sb], axis=1)],
                                  axis=0)
            x = dense_residual(mix, attn_w_out, li, x, mods3, mod_base, 2, layout)
        else:
            proj = dense(h, hgrn_w_in, li, F32)
            lb = lower_bounds[li]
            of_p, ob_p, st = hgrn_scan(proj, lb, 0, batch, seq, emit_state=True)
            of_s, ob_s = hgrn_scan(proj, lb, n_p, dec_batch, dec_seq, s0=state_hgrn[:, li])
            new_s.append(st)
            mix = jnp.concatenate([hgrn_post(of_p, ob_p, proj, 0, hgrn_norm_g[li]),
                                   hgrn_post(of_s, ob_s, proj, n_p, hgrn_norm_g[li])], axis=0)
            x = dense_residual(mix, hgrn_w_out, li, x, mods3, mod_base, 2, layout)
        x = moe(x, norm2_g[layer], mods3, mod_base, layer, moe_w_router[layer], moe_b_router[layer],
                moe_w_gu, moe_b_gu, moe_w_down, moe_b_down, layout)

    return (x[:n_p].reshape(batch, seq, d), x[n_p:].reshape(dec_batch, dec_seq, d),
            jnp.stack(new_ka, axis=1), jnp.stack(new_va, axis=1), jnp.stack(new_kb, axis=1),
            jnp.stack(new_vb, axis=1), jnp.stack(new_s, axis=1))
```

```python
import functools
import math

import jax
import jax.numpy as jnp
from jax import lax
from jax.experimental import pallas as pl
from jax.experimental.pallas import tpu as pltpu

F32 = jnp.float32
BF16 = jnp.bfloat16

HEAD_DIM = 128
A_HEADS = 16
A_KV_HEADS = 4
B_HEADS = 16
B_KV_HEADS = 4
WINDOW = 128
Q_BLOCK = 128
GRID_W = 64
ROPE_BASE = 10000.0
HGRN_HEADS = 32
HGRN_DK = 128
N_EXPERTS = 32
TOP_K = 4
SWIGLU_ALPHA = 1.702
SWIGLU_LIMIT = 7.0
N_MOD = 6
N_COND_ROWS = 8
EPS = 1e-6
LOG2_E = math.log2(math.e)
Q_PRESCALE = HEAD_DIM ** -0.5 * LOG2_E

VMEM_LIMIT = 48 * 1024 * 1024
VMEM_LIMIT_BIG = 56 * 1024 * 1024
HGRN_CHUNK = 128
MOE_BLOCK = 256
MOE_DOWN_TILE = 2048


def _params(sem, vmem=VMEM_LIMIT):
    return pltpu.CompilerParams(dimension_semantics=sem, vmem_limit_bytes=vmem)


def _cond_row(i, tm, n_prompt_tok, dec_seq):
    tok0 = i * tm
    return jnp.where(tok0 < n_prompt_tok, 0, 1 + (tok0 - n_prompt_tok) // dec_seq)


def _ada_kernel(c_ref, w_ref, b_ref, o_ref):
    c = c_ref[...]
    s = c * jax.nn.sigmoid(c)
    o_ref[...] = jnp.dot(s.astype(BF16), w_ref[...].astype(BF16), preferred_element_type=F32) + b_ref[...]


def ada_modulation(cond, ada_w, ada_b, tn=512):
    depth, d, n = ada_w.shape
    return pl.pallas_call(
        _ada_kernel,
        out_shape=jax.ShapeDtypeStruct((depth, N_COND_ROWS, n), F32),
        grid=(depth, n // tn),
        in_specs=[
            pl.BlockSpec((N_COND_ROWS, d), lambda l, j: (0, 0)),
            pl.BlockSpec((None, d, tn), lambda l, j: (l, 0, j)),
            pl.BlockSpec((None, 1, tn), lambda l, j: (l, 0, j)),
        ],
        out_specs=pl.BlockSpec((None, N_COND_ROWS, tn), lambda l, j: (l, 0, j)),
        compiler_params=_params(("arbitrary", "arbitrary")),
        name="ada_modulation",
    )(cond, ada_w, ada_b.reshape(depth, 1, n))


def _norm_mod(x, g, sc, sh):
    ms = jnp.mean(x * x, axis=-1, keepdims=True)
    y = x * lax.rsqrt(ms + EPS) * g
    return y * (1.0 + sc) + sh


def _norm_mod_kernel(x_ref, g_ref, sh_ref, sc_ref, o_ref):
    o_ref[...] = _norm_mod(x_ref[...], g_ref[...], sc_ref[0], sh_ref[0]).astype(o_ref.dtype)


def _mod_spec(d, mod_base, slot, tm, layout):
    return pl.BlockSpec((1, 1, d), lambda i: (mod_base + _cond_row(i, tm, *layout) * N_MOD + slot, 0, 0))


def norm_modulate(x, gain, mods3, mod_base, layout, tm=256):
    t, d = x.shape
    return pl.pallas_call(
        _norm_mod_kernel,
        out_shape=jax.ShapeDtypeStruct((t, d), BF16),
        grid=(t // tm,),
        in_specs=[
            pl.BlockSpec((tm, d), lambda i: (i, 0)),
            pl.BlockSpec((1, d), lambda i: (0, 0)),
            _mod_spec(d, mod_base, 0, tm, layout),
            _mod_spec(d, mod_base, 1, tm, layout),
        ],
        out_specs=pl.BlockSpec((tm, d), lambda i: (i, 0)),
        compiler_params=_params(("arbitrary",)),
        name="norm_modulate",
    )(x, gain.reshape(1, d), mods3, mods3)


def _norm_router_kernel(x_ref, g_ref, sh_ref, sc_ref, wr_ref, br_ref, hp_ref, idx_ref, gate_ref):
    h = _norm_mod(x_ref[...], g_ref[...], sc_ref[0], sh_ref[0])
    hp_ref[...] = _pack_halves(h)

    logits = jnp.dot(h, wr_ref[...], precision=lax.Precision.HIGHEST, preferred_element_type=F32) + br_ref[...]
    n_e = logits.shape[1]
    lane = lax.broadcasted_iota(jnp.int32, logits.shape, 1).astype(F32)
    vals = logits
    top_v, top_i = [], []
    for _ in range(TOP_K):
        m = jnp.max(vals, axis=-1, keepdims=True)
        first = jnp.min(jnp.where(vals == m, lane, float(n_e)), axis=-1, keepdims=True)
        vals = jnp.where(lane == first, -jnp.inf, vals)
        top_v.append(m)
        top_i.append(first)
    ex = [jnp.exp(v - top_v[0]) for v in top_v]
    denom = ex[0] + ex[1] + ex[2] + ex[3]
    for k in range(TOP_K):
        idx_ref[:, k:k + 1] = top_i[k].astype(jnp.int32)
        gate_ref[:, k:k + 1] = ex[k] / denom


def norm_modulate_route(x, gain, mods3, mod_base, w_router, b_router, layout, tm=256):
    t, d = x.shape
    n_e = w_router.shape[1]
    return pl.pallas_call(
        _norm_router_kernel,
        out_shape=(
            jax.ShapeDtypeStruct((t, d // 2), jnp.uint32),
            jax.ShapeDtypeStruct((t, TOP_K), jnp.int32),
            jax.ShapeDtypeStruct((t, TOP_K), F32),
        ),
        grid=(t // tm,),
        in_specs=[
            pl.BlockSpec((tm, d), lambda i: (i, 0)),
            pl.BlockSpec((1, d), lambda i: (0, 0)),
            _mod_spec(d, mod_base, 3, tm, layout),
            _mod_spec(d, mod_base, 4, tm, layout),
            pl.BlockSpec((d, n_e), lambda i: (0, 0)),
            pl.BlockSpec((1, n_e), lambda i: (0, 0)),
        ],
        out_specs=(
            pl.BlockSpec((tm, d // 2), lambda i: (i, 0)),
            pl.BlockSpec((tm, TOP_K), lambda i: (i, 0)),
            pl.BlockSpec((tm, TOP_K), lambda i: (i, 0)),
        ),
        compiler_params=_params(("arbitrary",)),
        name="norm_modulate_route",
    )(x, gain.reshape(1, d), mods3, mods3, w_router, b_router.reshape(1, n_e))


def _dense_kernel(a_ref, w_ref, o_ref, wbf_ref):
    @pl.when(pl.program_id(1) == 0)
    def _():
        wbf_ref[...] = w_ref[...].astype(BF16)

    o_ref[...] = jnp.dot(a_ref[...], wbf_ref[...], preferred_element_type=F32).astype(o_ref.dtype)


def _dense_residual_kernel(a_ref, w_ref, x_ref, gate_ref, o_ref, wbf_ref):
    @pl.when(pl.program_id(1) == 0)
    def _():
        wbf_ref[...] = w_ref[...].astype(BF16)

    o_ref[...] = x_ref[...] + gate_ref[0] * jnp.dot(a_ref[...], wbf_ref[...], preferred_element_type=F32)


def dense(a, w3, layer, out_dtype, tm=512, tn=512):
    t, k = a.shape
    n = w3.shape[2]
    return pl.pallas_call(
        _dense_kernel,
        out_shape=jax.ShapeDtypeStruct((t, n), out_dtype),
        grid=(n // tn, t // tm),
        in_specs=[
            pl.BlockSpec((tm, k), lambda j, i: (i, 0)),
            pl.BlockSpec((None, k, tn), lambda j, i: (layer, 0, j)),
        ],
        out_specs=pl.BlockSpec((tm, tn), lambda j, i: (i, j)),
        scratch_shapes=[pltpu.VMEM((k, tn), BF16)],
        compiler_params=_params(("arbitrary", "arbitrary")),
        name="dense",
    )(a, w3)


def dense_residual(a, w3, layer, x, mods3, mod_base, slot, layout, tm=512, tn=512):
    t, k = a.shape
    n = w3.shape[2]
    return pl.pallas_call(
        _dense_residual_kernel,
        out_shape=jax.ShapeDtypeStruct((t, n), F32),
        grid=(n // tn, t // tm),
        in_specs=[
            pl.BlockSpec((tm, k), lambda j, i: (i, 0)),
            pl.BlockSpec((None, k, tn), lambda j, i: (layer, 0, j)),
            pl.BlockSpec((tm, tn), lambda j, i: (i, j)),
            pl.BlockSpec((1, 1, tn), lambda j, i: (mod_base + _cond_row(i, tm, *layout) * N_MOD + slot, 0, j)),
        ],
        out_specs=pl.BlockSpec((tm, tn), lambda j, i: (i, j)),
        scratch_shapes=[pltpu.VMEM((k, tn), BF16)],
        input_output_aliases={2: 0},
        compiler_params=_params(("arbitrary", "arbitrary")),
        name="dense_residual",
    )(a, w3, x, mods3)


def _head_rmsnorm(x, g):
    return x * lax.rsqrt(jnp.mean(x * x, axis=-1, keepdims=True) + EPS) * g


def _qkv_post_kernel(qkv_ref, cos_ref, sin_ref, qna_ref, kna_ref, qnb_ref, knb_ref, o_ref, kn_ref):
    cos = cos_ref[...]
    sin = sin_ref[...]
    even = (lax.broadcasted_iota(jnp.int32, cos.shape, 1) & 1) == 0
    segs = ((A_HEADS, qna_ref, False), (A_KV_HEADS, kna_ref, True), (A_KV_HEADS, None, False),
            (B_HEADS, qnb_ref, False), (B_KV_HEADS, knb_ref, True), (B_KV_HEADS, None, False))
    head = 0
    k_head = 0
    for n_heads, gain_ref, is_key in segs:
        for _ in range(n_heads):
            sl = slice(head * HEAD_DIM, (head + 1) * HEAD_DIM)
            x = qkv_ref[:, sl]
            if gain_ref is not None:
                x = _head_rmsnorm(x, gain_ref[...])
                if is_key:
                    kn_ref[:, k_head * HEAD_DIM:(k_head + 1) * HEAD_DIM] = x
                    k_head += 1
                partner = jnp.where(even, pltpu.roll(x, HEAD_DIM - 1, axis=1), pltpu.roll(x, 1, axis=1))
                x = x * cos + partner * sin
                if not is_key:
                    x = x * Q_PRESCALE
            o_ref[:, sl] = x.astype(o_ref.dtype)
            head += 1


def qkv_post(qkv, cos, sin, qn_a, kn_a, qn_b, kn_b, tm=256):
    t, n = qkv.shape
    n_k = (A_KV_HEADS + B_KV_HEADS) * HEAD_DIM
    gain = lambda g: g.reshape(1, HEAD_DIM)
    gspec = pl.BlockSpec((1, HEAD_DIM), lambda i: (0, 0))
    return pl.pallas_call(
        _qkv_post_kernel,
        out_shape=(jax.ShapeDtypeStruct((t, n), BF16), jax.ShapeDtypeStruct((t, n_k), F32)),
        grid=(t // tm,),
        in_specs=[
            pl.BlockSpec((tm, n), lambda i: (i, 0)),
            pl.BlockSpec((tm, HEAD_DIM), lambda i: (i, 0)),
            pl.BlockSpec((tm, HEAD_DIM), lambda i: (i, 0)),
            gspec, gspec, gspec, gspec,
        ],
        out_specs=(pl.BlockSpec((tm, n), lambda i: (i, 0)), pl.BlockSpec((tm, n_k), lambda i: (i, 0))),
        compiler_params=_params(("arbitrary",)),
        name="qkv_post",
    )(qkv, cos, sin, gain(qn_a), gain(kn_a), gain(qn_b), gain(kn_b))


def _attn_kernel(*refs, rep, has_cache, window, has_sink):
    it = iter(refs)
    q_ref, k_ref, v_ref = next(it), next(it), next(it)
    kc_ref, vc_ref = (next(it), next(it)) if has_cache else (None, None)
    sink_ref = next(it) if has_sink else None
    next(it)
    o_ref = next(it)

    qi = pl.program_id(2)
    q = q_ref[...]
    qs = jnp.concatenate([q[:, r * HEAD_DIM:(r + 1) * HEAD_DIM] for r in range(rep)], axis=0)
    nt = (((1,), (1,)), ((), ()))
    if window:
        span = Q_BLOCK + 2 * WINDOW
        n_local = k_ref.shape[0]
        start = pl.multiple_of(jnp.clip(qi * Q_BLOCK - WINDOW, 0, n_local - span), Q_BLOCK)
        k1 = k_ref[pl.ds(start, span), :]
        v1 = v_ref[pl.ds(start, span), :]
    else:
        k1 = k_ref[...]
        v1 = v_ref[...]
    s1 = lax.dot_general(qs, k1, nt, preferred_element_type=F32)
    if window:
        qpos = qi * Q_BLOCK + (lax.broadcasted_iota(jnp.int32, s1.shape, 0) & (Q_BLOCK - 1))
        kpos = start + lax.broadcasted_iota(jnp.int32, s1.shape, 1)
        s1 = jnp.where(jnp.abs(qpos - kpos) <= WINDOW, s1, -jnp.inf)
    m = jnp.max(s1, axis=-1, keepdims=True)
    if has_cache:
        s2 = lax.dot_general(qs, kc_ref[...], nt, preferred_element_type=F32)
        m = jnp.maximum(m, jnp.max(s2, axis=-1, keepdims=True))
    if has_sink:
        sink = sink_ref[...] * LOG2_E
        m = jnp.maximum(m, sink)
    p1 = jnp.exp2(s1 - m)
    denom = jnp.sum(p1, axis=-1, keepdims=True)
    acc = jnp.dot(p1.astype(BF16), v1, preferred_element_type=F32)
    if has_cache:
        p2 = jnp.exp2(s2 - m)
        denom = denom + jnp.sum(p2, axis=-1, keepdims=True)
        acc = acc + jnp.dot(p2.astype(BF16), vc_ref[...], preferred_element_type=F32)
    if has_sink:
        denom = denom + jnp.exp2(sink - m)
    o = acc / denom
    o_ref[...] = jnp.concatenate([o[r * Q_BLOCK:(r + 1) * Q_BLOCK] for r in range(rep)], axis=1).astype(o_ref.dtype)


def attention(qkv, q_col, k_col, v_col, n_heads, n_kv, row0, n_seq, seq_len, out, out_col, cache_k=None,
              cache_v=None, sink=None, window=False):
    rep = n_heads // n_kv
    qw = rep * HEAD_DIM
    nqb = seq_len // Q_BLOCK
    rb0 = row0 // Q_BLOCK
    sb0 = row0 // seq_len
    in_specs = [
        pl.BlockSpec((Q_BLOCK, qw), lambda b, g, i: (rb0 + b * nqb + i, q_col // qw + g)),
        pl.BlockSpec((seq_len, HEAD_DIM), lambda b, g, i: (sb0 + b, k_col // HEAD_DIM + g)),
        pl.BlockSpec((seq_len, HEAD_DIM), lambda b, g, i: (sb0 + b, v_col // HEAD_DIM + g)),
    ]
    args = [qkv, qkv, qkv]
    if cache_k is not None:
        n_past = cache_k.shape[1]
        spec = pl.BlockSpec((None, n_past, HEAD_DIM), lambda b, g, i: (b, 0, g))
        in_specs += [spec, spec]
        args += [cache_k, cache_v]
    if sink is not None:
        sink_rows = jnp.repeat(sink.astype(F32), Q_BLOCK).reshape(n_kv * rep * Q_BLOCK, 1)
        in_specs.append(pl.BlockSpec((rep * Q_BLOCK, 1), lambda b, g, i: (g, 0)))
        args.append(sink_rows)
    aliases = {len(args): 0}
    in_specs.append(pl.BlockSpec(memory_space=pl.ANY))
    args.append(out)
    return pl.pallas_call(
        functools.partial(_attn_kernel, rep=rep, has_cache=cache_k is not None, window=window,
                          has_sink=sink is not None),
        out_shape=jax.ShapeDtypeStruct(out.shape, out.dtype),
        grid=(n_seq, n_kv, nqb),
        in_specs=in_specs,
        out_specs=pl.BlockSpec((Q_BLOCK, qw), lambda b, g, i: (rb0 + b * nqb + i, out_col // qw + g)),
        input_output_aliases=aliases,
        compiler_params=_params(("arbitrary", "arbitrary", "arbitrary")),
        name="attention",
    )(*args)


def _hgrn_chunk(q, z, v, lb, st, ups, pair_ref, forward):
    c = q.shape[0]
    e = jnp.exp(-jnp.abs(z))
    r = 1.0 / (1.0 + e)
    pos = z >= 0
    sig_pos = jnp.where(pos, r, e * r)
    sig_neg = jnp.where(pos, e * r, r)
    f = lb + (1.0 - lb) * sig_pos
    logf = jnp.log(f)
    k = (1.0 - lb) * sig_neg
    nt = (((1,), (1,)), ((), ()))

    a = logf if forward else jnp.zeros_like(logf)
    tot = logf
    att = None
    for lvl, up in enumerate(ups):
        half = 1 << lvl
        if half == 1:
            ex = jnp.where(up, f, 1.0) if forward else jnp.where(up, 1.0, f)
        else:
            ex = jnp.exp(jnp.where(up, a, tot - a))
        pairs = lax.dot_general((q * ex).astype(BF16), (k * ex).astype(BF16), nt, preferred_element_type=F32)
        pairs = pairs * pair_ref[lvl]
        att = pairs if att is None else att + pairs
        tot_dn = pltpu.roll(tot, half, axis=0)
        tot_up = pltpu.roll(tot, c - half, axis=0)
        a = a + jnp.where(up, tot_dn, 0.0)
        tot = tot + jnp.where(up, tot_dn, tot_up)

    q_arg, k_arg = (a, tot - a) if forward else (tot - a, a)
    vb = v.astype(BF16)
    o = lax.dot_general((q * jnp.exp(q_arg)).astype(BF16), st.astype(BF16), nt, preferred_element_type=F32)
    o = o + jnp.dot(att.astype(BF16), vb, preferred_element_type=F32)
    o = o + jnp.sum(q * k, axis=-1, keepdims=True) * v
    upd = lax.dot_general(vb, (k * jnp.exp(k_arg)).astype(BF16), (((0,), (0,)), ((), ())),
                          preferred_element_type=F32)
    st_new = jnp.exp(tot[0:1, :]) * st + upd
    return o, st_new


def _hgrn_scan_kernel(*refs, hpg, has_s0, emit_state):
    it = iter(refs)
    qf, zf, vf, qb, zb, vb, lbf, lbb = [next(it) for _ in range(8)]
    s0 = next(it) if has_s0 else None
    of, ob = next(it), next(it)
    sfin = next(it) if emit_state else None
    st, pair_ref = next(it), next(it)
    step = pl.program_id(2)
    c = qf.shape[0]
    levels = pair_ref.shape[1]

    @pl.when(step == 0)
    def _init():
        if has_s0:
            for d in range(2):
                for h in range(hpg):
                    st[d, h] = s0[0, d, h].T
        else:
            st[...] = jnp.zeros(st.shape, F32)
        rowc = lax.broadcasted_iota(jnp.int32, (c, c), 0)
        colc = lax.broadcasted_iota(jnp.int32, (c, c), 1)
        for lvl in range(levels):
            siblings = ((rowc ^ colc) >> lvl) == 1
            q_upper = (rowc & (1 << lvl)) != 0
            pair_ref[0, lvl] = jnp.where(jnp.logical_and(siblings, q_upper), 1.0, 0.0)
            pair_ref[1, lvl] = jnp.where(jnp.logical_and(siblings, jnp.logical_not(q_upper)), 1.0, 0.0)

    row = lax.broadcasted_iota(jnp.int32, (c, HGRN_DK), 0)
    ups = [(row & (1 << lvl)) != 0 for lvl in range(levels)]
    for h in range(hpg):
        sl = slice(h * HGRN_DK, (h + 1) * HGRN_DK)
        for d, (q_r, z_r, v_r, lb_r, o_r) in enumerate(((qf, zf, vf, lbf, of), (qb, zb, vb, lbb, ob))):
            o, st_new = _hgrn_chunk(q_r[:, sl], z_r[:, sl], v_r[:, sl], lb_r[:, sl], st[d, h], ups, pair_ref.at[d],
                                    forward=(d == 0))
            o_r[:, sl] = o
            st[d, h] = st_new

    if emit_state:
        @pl.when(step == pl.num_programs(2) - 1)
        def _fin():
            for d in range(2):
                for h in range(hpg):
                    sfin[0, d, h] = st[d, h].T


def hgrn_scan(proj, lb, row0, n_seq, seq_len, s0=None, emit_state=False, chunk=HGRN_CHUNK, hpg=4):
    hk = HGRN_HEADS * HGRN_DK
    hw = hpg * HGRN_DK
    n_hg = HGRN_HEADS // hpg
    nc = seq_len // chunk
    rb0 = row0 // chunk
    seg = hk // hw

    def fwd(col):
        return pl.BlockSpec((chunk, hw), lambda b, g, c: (rb0 + b * nc + c, col * seg + g))

    def bwd(col):
        return pl.BlockSpec((chunk, hw), lambda b, g, c: (rb0 + b * nc + nc - 1 - c, col * seg + g))

    in_specs = [fwd(0), fwd(1), fwd(3), bwd(0), bwd(2), bwd(3),
                pl.BlockSpec((1, hw), lambda b, g, c: (0, g)), pl.BlockSpec((1, hw), lambda b, g, c: (0, g))]
    args = [proj, proj, proj, proj, proj, proj, lb[0:1], lb[1:2]]
    state_spec = pl.BlockSpec((1, 2, hpg, HGRN_DK, HGRN_DK), lambda b, g, c: (b, 0, g, 0, 0))
    if s0 is not None:
        in_specs.append(state_spec)
        args.append(s0)
    n_rows = n_seq * seq_len
    out_shape = [jax.ShapeDtypeStruct((n_rows, hk), F32), jax.ShapeDtypeStruct((n_rows, hk), F32)]
    out_specs = [pl.BlockSpec((chunk, hw), lambda b, g, c: (b * nc + c, g)),
                 pl.BlockSpec((chunk, hw), lambda b, g, c: (b * nc + nc - 1 - c, g))]
    if emit_state:
        out_shape.append(jax.ShapeDtypeStruct((n_seq, 2, HGRN_HEADS, HGRN_DK, HGRN_DK), F32))
        out_specs.append(state_spec)
    return pl.pallas_call(
        functools.partial(_hgrn_scan_kernel, hpg=hpg, has_s0=s0 is not None, emit_state=emit_state),
        out_shape=tuple(out_shape),
        grid=(n_seq, n_hg, nc),
        in_specs=in_specs,
        out_specs=tuple(out_specs),
        scratch_shapes=[pltpu.VMEM((2, hpg, HGRN_DK, HGRN_DK), F32),
                        pltpu.VMEM((2, int(math.log2(chunk)), chunk, chunk), F32)],
        compiler_params=_params(("arbitrary", "arbitrary", "arbitrary")),
        name="hgrn_scan",
    )(*args)


def _hgrn_post_kernel(of_ref, ob_ref, g_ref, ng_ref, _, o_ref):
    for h in range(HGRN_HEADS):
        sl = slice(h * HGRN_DK, (h + 1) * HGRN_DK)
        o = of_ref[:, sl] + ob_ref[:, sl]
        g = g_ref[:, sl]
        y = _head_rmsnorm(o, ng_ref[:, sl]) * (g * jax.nn.sigmoid(g))
        o_ref[:, sl] = y.astype(o_ref.dtype)


def hgrn_post(o_f, o_b, proj, row0, norm_g, out, tm=256):
    t, hk = o_f.shape
    rb0 = row0 // tm
    spec = pl.BlockSpec((tm, hk), lambda i: (i, 0))
    return pl.pallas_call(
        _hgrn_post_kernel,
        out_shape=jax.ShapeDtypeStruct(out.shape, out.dtype),
        grid=(t // tm,),
        in_specs=[spec, spec, pl.BlockSpec((tm, hk), lambda i: (rb0 + i, 4)), pl.BlockSpec((1, hk), lambda i: (0, 0)),
                  pl.BlockSpec(memory_space=pl.ANY)],
        out_specs=pl.BlockSpec((tm, hk), lambda i: (rb0 + i, 0)),
        input_output_aliases={4: 0},
        compiler_params=_params(("arbitrary",)),
        name="hgrn_post",
    )(o_f, o_b, proj, norm_g.reshape(1, hk), out)


def moe_layout(idx, block):
    n_assign = idx.size
    flat_e = idx.reshape(-1)
    pos = jnp.arange(n_assign, dtype=jnp.int32)
    experts = jnp.arange(N_EXPERTS, dtype=jnp.int32)
    e_sorted, order = lax.sort((flat_e, pos), num_keys=1, is_stable=True)
    counts = jnp.sum((flat_e[:, None] == experts[None, :]).astype(jnp.int32), axis=0)
    padded = (counts + block - 1) // block * block
    pad_end = jnp.cumsum(padded)
    pad_start = pad_end - padded
    start = jnp.cumsum(counts) - counts
    n_blocks = -(-(n_assign + N_EXPERTS * (block - 1)) // block)
    first_slot = jnp.arange(n_blocks, dtype=jnp.int32) * block
    block_e = jnp.minimum(jnp.sum((pad_end[None, :] <= first_slot[:, None]).astype(jnp.int32), axis=1), N_EXPERTS - 1)
    block_off = first_slot - pad_start[block_e]
    block_cnt = jnp.clip(counts[block_e] - block_off, 0, block)
    block_src = start[block_e] + block_off
    n_used = (pad_end[-1:] // block).astype(jnp.int32)
    shift = jnp.sum(jnp.where(e_sorted[:, None] == experts[None, :], (pad_start - start)[None, :], 0), axis=1)
    _, slot_of = lax.sort((order, pos + shift), num_keys=1)
    return order, slot_of, block_e, block_src, block_cnt, n_used


def _pack_halves(x):
    half = x.shape[1] // 2
    bits = lax.bitcast_convert_type(x.astype(BF16).astype(F32), jnp.uint32)
    return (bits[:, half:] & jnp.uint32(0xFFFF0000)) | (bits[:, :half] >> 16)


def _unpack_halves(w):
    return lax.bitcast_convert_type(w << 16, F32), lax.bitcast_convert_type(w & jnp.uint32(0xFFFF0000), F32)


def _gather_kernel(order_ref, src_ref, cnt_ref, hp_hbm, o_ref, sem):
    rows = o_ref.shape[0]
    i = pl.program_id(0)
    last_assign = order_ref.shape[0] - 1
    src0 = src_ref[i]
    cnt = cnt_ref[i]

    def issue(r, carry):
        assign = order_ref[jnp.minimum(src0 + r, last_assign)]
        tok = jnp.where(r < cnt, assign // TOP_K, 0)
        pltpu.make_async_copy(hp_hbm.at[tok], o_ref.at[r], sem).start()
        return carry

    lax.fori_loop(0, rows, issue, 0, unroll=8)
    pltpu.make_async_copy(hp_hbm.at[pl.ds(0, rows)], o_ref, sem).wait()


def moe_gather(order, block_src, block_cnt, hp, rows):
    n_slots = block_src.shape[0] * rows
    half = hp.shape[1]
    return pl.pallas_call(
        _gather_kernel,
        out_shape=jax.ShapeDtypeStruct((n_slots, half), hp.dtype),
        grid_spec=pltpu.PrefetchScalarGridSpec(
            num_scalar_prefetch=3,
            grid=(n_slots // rows,),
            in_specs=[pl.BlockSpec(memory_space=pl.ANY)],
            out_specs=pl.BlockSpec((rows, half), lambda i, order, src, cnt: (i, 0)),
            scratch_shapes=[pltpu.SemaphoreType.DMA],
        ),
        compiler_params=_params(("arbitrary",)),
        name="moe_gather",
    )(order, block_src, block_cnt, hp)


def _first_block_of_expert(be_ref, b):
    return jnp.logical_or(b == 0, be_ref[b] != be_ref[jnp.maximum(b - 1, 0)])


def _used_block(b, nused_ref):
    return jnp.minimum(b, nused_ref[0] - 1)


def _moe_gu_kernel(be_ref, nused_ref, x_ref, wg_ref, wu_ref, bg_ref, bu_ref, o_ref, wg_bf, wu_bf):
    b = pl.program_id(1)

    @pl.when(b < nused_ref[0])
    def _():
        @pl.when(_first_block_of_expert(be_ref, b))
        def _():
            wg_bf[...] = wg_ref[...].astype(BF16)
            wu_bf[...] = wu_ref[...].astype(BF16)

        lo, hi = _unpack_halves(x_ref[...])
        lo = lo.astype(BF16)
        hi = hi.astype(BF16)
        half = lo.shape[1]

        def proj(w_bf, bias_ref):
            return (jnp.dot(lo, w_bf[:half, :], preferred_element_type=F32)
                    + jnp.dot(hi, w_bf[half:, :], preferred_element_type=F32) + bias_ref[...])

        gate = jnp.minimum(proj(wg_bf, bg_ref), SWIGLU_LIMIT)
        up = jnp.clip(proj(wu_bf, bu_ref), -SWIGLU_LIMIT, SWIGLU_LIMIT)
        o_ref[...] = (gate * jax.nn.sigmoid(SWIGLU_ALPHA * gate) * (up + 1.0)).astype(o_ref.dtype)

    @pl.when(b >= nused_ref[0])
    def _():
        o_ref[...] = jnp.zeros(o_ref.shape, o_ref.dtype)


def moe_gate_up(block_e, n_used, x_sorted, w_gu, b_gu, layer, block, tj=512):
    n_slots, half = x_sorted.shape
    d = 2 * half
    de = w_gu.shape[3] // 2
    n_j = de // tj
    b4 = b_gu.reshape(b_gu.shape[0], N_EXPERTS, 1, 2 * de)

    def w_spec(col0):
        return pl.BlockSpec((None, None, d, tj), lambda j, b, be, nu: (layer, be[_used_block(b, nu)], 0, col0 + j))

    def b_spec(col0):
        return pl.BlockSpec((None, None, 1, tj), lambda j, b, be, nu: (layer, be[_used_block(b, nu)], 0, col0 + j))

    return pl.pallas_call(
        _moe_gu_kernel,
        out_shape=jax.ShapeDtypeStruct((n_slots, de), BF16),
        grid_spec=pltpu.PrefetchScalarGridSpec(
            num_scalar_prefetch=2,
            grid=(n_j, n_slots // block),
            in_specs=[
                pl.BlockSpec((block, half), lambda j, b, be, nu: (_used_block(b, nu), 0)),
                w_spec(0), w_spec(n_j), b_spec(0), b_spec(n_j),
            ],
            out_specs=pl.BlockSpec((block, tj), lambda j, b, be, nu: (b, j)),
            scratch_shapes=[pltpu.VMEM((d, tj), BF16), pltpu.VMEM((d, tj), BF16)],
        ),
        compiler_params=_params(("arbitrary", "arbitrary"), VMEM_LIMIT_BIG),
        name="moe_gate_up",
    )(block_e, n_used, x_sorted, w_gu, w_gu, b4, b4)


def _moe_down_kernel(be_ref, nused_ref, a_ref, w_ref, b_ref, o_ref, w_bf):
    b = pl.program_id(1)

    @pl.when(b < nused_ref[0])
    def _():
        @pl.when(_first_block_of_expert(be_ref, b))
        def _():
            w_bf[...] = w_ref[...].astype(BF16)

        o_ref[...] = _pack_halves(jnp.dot(a_ref[...], w_bf[...], preferred_element_type=F32) + b_ref[...])

    @pl.when(b >= nused_ref[0])
    def _():
        o_ref[...] = jnp.zeros(o_ref.shape, o_ref.dtype)


def moe_down(block_e, n_used, act, w_down, b_down, layer, block, tn):
    n_slots, de = act.shape
    d = w_down.shape[3]
    b4 = b_down.reshape(b_down.shape[0], N_EXPERTS, 1, d)
    return pl.pallas_call(
        _moe_down_kernel,
        out_shape=jax.ShapeDtypeStruct((n_slots, d // 2), jnp.uint32),
        grid_spec=pltpu.PrefetchScalarGridSpec(
            num_scalar_prefetch=2,
            grid=(d // tn, n_slots // block),
            in_specs=[
                pl.BlockSpec((block, de), lambda n, b, be, nu: (_used_block(b, nu), 0)),
                pl.BlockSpec((None, None, de, tn), lambda n, b, be, nu: (layer, be[_used_block(b, nu)], 0, n)),
                pl.BlockSpec((None, None, 1, tn), lambda n, b, be, nu: (layer, be[_used_block(b, nu)], 0, n)),
            ],
            out_specs=pl.BlockSpec((block, tn // 2), lambda n, b, be, nu: (b, n)),
            scratch_shapes=[pltpu.VMEM((de, tn), BF16)],
        ),
        compiler_params=_params(("arbitrary", "arbitrary")),
        name="moe_down",
    )(block_e, n_used, act, w_down, b4)


def _combine_kernel(slot_ref, y_hbm, x_ref, gate_ref, g2_ref, o_ref, buf, sem, *, tn):
    tb, half_d = buf.shape[2], buf.shape[3]
    i = pl.program_id(0)

    def issue(step, slot):
        base = step * tb * TOP_K

        def body(t, carry):
            for k in range(TOP_K):
                pltpu.make_async_copy(y_hbm.at[slot_ref[base + t * TOP_K + k]], buf.at[slot, k, t],
                                      sem.at[slot]).start()
            return carry

        lax.fori_loop(0, tb, body, 0, unroll=2)

    @pl.when(i == 0)
    def _():
        issue(0, 0)

    @pl.when(i + 1 < pl.num_programs(0))
    def _():
        issue(i + 1, (i + 1) % 2)

    slot = i % 2
    for k in range(TOP_K):
        pltpu.make_async_copy(y_hbm.at[pl.ds(0, tb)], buf.at[slot, k], sem.at[slot]).wait()
    gate = gate_ref[...]
    hw = tn // 2
    for tile in range(2 * half_d // tn):
        acc_lo = acc_hi = None
        for k in range(TOP_K):
            lo, hi = _unpack_halves(buf[slot, k, :, tile * hw:(tile + 1) * hw])
            g = gate[:, k:k + 1]
            acc_lo = g * lo if acc_lo is None else acc_lo + g * lo
            acc_hi = g * hi if acc_hi is None else acc_hi + g * hi
        for acc, c0 in ((acc_lo, tile * tn), (acc_hi, tile * tn + hw)):
            o_ref[:, c0:c0 + hw] = x_ref[:, c0:c0 + hw] + g2_ref[0][:, c0:c0 + hw] * acc


def moe_combine(slot_of, y, x, gate, mods3, mod_base, layout, tn, tb=64):
    t, d = x.shape
    return pl.pallas_call(
        functools.partial(_combine_kernel, tn=tn),
        out_shape=jax.ShapeDtypeStruct((t, d), F32),
        grid_spec=pltpu.PrefetchScalarGridSpec(
            num_scalar_prefetch=1,
            grid=(t // tb,),
            in_specs=[
                pl.BlockSpec(memory_space=pl.ANY),
                pl.BlockSpec((tb, d), lambda i, s: (i, 0)),
                pl.BlockSpec((tb, TOP_K), lambda i, s: (i, 0)),
                pl.BlockSpec((1, 1, d), lambda i, s: (mod_base + _cond_row(i, tb, *layout) * N_MOD + 5, 0, 0)),
            ],
            out_specs=pl.BlockSpec((tb, d), lambda i, s: (i, 0)),
            scratch_shapes=[pltpu.VMEM((2, TOP_K, tb, d // 2), jnp.uint32), pltpu.SemaphoreType.DMA((2,))],
        ),
        input_output_aliases={2: 0},
        compiler_params=_params(("arbitrary",)),
        name="moe_combine",
    )(slot_of, y, x, gate, mods3)


def moe(x, gain, mods3, mod_base, layer, w_router, b_router, w_gu, b_gu, w_down, b_down, layout):
    hp, idx, gate = norm_modulate_route(x, gain, mods3, mod_base, w_router, b_router, layout)
    order, slot_of, block_e, block_src, block_cnt, n_used = moe_layout(idx, MOE_BLOCK)
    x_sorted = moe_gather(order, block_src, block_cnt, hp, MOE_BLOCK)
    act = moe_gate_up(block_e, n_used, x_sorted, w_gu, b_gu, layer, MOE_BLOCK)
    tn = min(MOE_DOWN_TILE, w_down.shape[3])
    y = moe_down(block_e, n_used, act, w_down, b_down, layer, MOE_BLOCK, tn)
    return moe_combine(slot_of, y, x, gate, mods3, mod_base, layout, tn)


def _rope_tables(n_prompt_tok, dec_batch, dec_seq):
    pos = jnp.arange(dec_seq)
    row = (pos // GRID_W).astype(F32)
    col = (pos % GRID_W).astype(F32)
    n_freq = HEAD_DIM // 4
    inv_freq = ROPE_BASE ** (-jnp.arange(n_freq, dtype=F32) / n_freq)
    ang = jnp.concatenate([row[:, None] * inv_freq, col[:, None] * inv_freq], axis=-1)
    cos = jnp.repeat(jnp.cos(ang), 2, axis=-1)
    sin = jnp.repeat(jnp.sin(ang), 2, axis=-1) * jnp.tile(jnp.array([-1.0, 1.0], F32), HEAD_DIM // 2)
    cos = jnp.concatenate([jnp.ones((n_prompt_tok, HEAD_DIM), F32), jnp.tile(cos, (dec_batch, 1))], axis=0)
    sin = jnp.concatenate([jnp.zeros((n_prompt_tok, HEAD_DIM), F32), jnp.tile(sin, (dec_batch, 1))], axis=0)
    return cos, sin


def kernel(x_prompt, x_sample, c, c_ctx, cache_k_a, cache_v_a, cache_k_b, cache_v_b, state_hgrn,
           ada_w, ada_b, norm1_g, norm2_g, attn_w_in, attn_qn_a, attn_kn_a, attn_qn_b, attn_kn_b,
           attn_sink_b, attn_w_out, hgrn_w_in, hgrn_lb_logits, hgrn_norm_g, hgrn_w_out,
           moe_w_router, moe_b_router, moe_w_gu, moe_b_gu, moe_w_down, moe_b_down):
    batch, seq, d = x_prompt.shape
    dec_batch, dec_seq, _ = x_sample.shape
    depth = ada_w.shape[0]
    n_p = batch * seq
    n_s = dec_batch * dec_seq
    layout = (n_p, dec_seq)
    assert 1 + dec_batch <= N_COND_ROWS

    cond = jnp.zeros((N_COND_ROWS, d), F32).at[0].set(c_ctx).at[1:1 + dec_batch].set(c)
    mods3 = ada_modulation(cond, ada_w, ada_b).reshape(depth * N_COND_ROWS * N_MOD, 1, d)
    x = jnp.concatenate([x_prompt.reshape(n_p, d), x_sample.reshape(n_s, d)], axis=0)
    cos, sin = _rope_tables(n_p, dec_batch, dec_seq)
    p_lb = jax.nn.softmax(hgrn_lb_logits.astype(F32), axis=0)
    lower_bounds = jnp.cumsum(p_lb, axis=0) - p_lb[0]

    qa_col = 0
    ka_col = qa_col + A_HEADS * HEAD_DIM
    va_col = ka_col + A_KV_HEADS * HEAD_DIM
    qb_col = va_col + A_KV_HEADS * HEAD_DIM
    kb_col = qb_col + B_HEADS * HEAD_DIM
    vb_col = kb_col + B_KV_HEADS * HEAD_DIM
    kv_a = A_KV_HEADS * HEAD_DIM
    kv_b = B_KV_HEADS * HEAD_DIM

    new_ka, new_va, new_kb, new_vb, new_s = [], [], [], [], []
    for layer in range(depth):
        mod_base = layer * N_COND_ROWS * N_MOD
        li = layer // 2
        h = norm_modulate(x, norm1_g[layer], mods3, mod_base, layout)
        if layer % 2 == 0:
            qkv = dense(h, attn_w_in, li, F32)
            qkv_bf, kn = qkv_post(qkv, cos, sin, attn_qn_a[li], attn_kn_a[li], attn_qn_b[li], attn_kn_b[li])
            new_ka.append(kn[:n_p, :kv_a].reshape(batch, seq, A_KV_HEADS, HEAD_DIM))
            new_kb.append(kn[:n_p, kv_a:].reshape(batch, seq, B_KV_HEADS, HEAD_DIM))
            new_va.append(qkv[:n_p, va_col:va_col + kv_a].reshape(batch, seq, A_KV_HEADS, HEAD_DIM))
            new_vb.append(qkv[:n_p, vb_col:vb_col + kv_b].reshape(batch, seq, B_KV_HEADS, HEAD_DIM))
            n_past = cache_k_a.shape[2]
            cache = lambda t, nkv: t[:, li].reshape(dec_batch, n_past, nkv * HEAD_DIM).astype(BF16)
            sink = attn_sink_b[li]
            b_col = A_HEADS * HEAD_DIM
            mix = jnp.zeros((n_p + n_s, (A_HEADS + B_HEADS) * HEAD_DIM), BF16)
            mix = attention(qkv_bf, qa_col, ka_col, va_col, A_HEADS, A_KV_HEADS, 0, batch, seq, mix, 0)
            mix = attention(qkv_bf, qb_col, kb_col, vb_col, B_HEADS, B_KV_HEADS, 0, batch, seq, mix, b_col, sink=sink)
            mix = attention(qkv_bf, qa_col, ka_col, va_col, A_HEADS, A_KV_HEADS, n_p, dec_batch, dec_seq, mix, 0,
                            cache_k=cache(cache_k_a, A_KV_HEADS), cache_v=cache(cache_v_a, A_KV_HEADS))
            mix = attention(qkv_bf, qb_col, kb_col, vb_col, B_HEADS, B_KV_HEADS, n_p, dec_batch, dec_seq, mix, b_col,
                            cache_k=cache(cache_k_b, B_KV_HEADS), cache_v=cache(cache_v_b, B_KV_HEADS),
                            sink=sink, window=True)
            x = dense_residual(mix, attn_w_out, li, x, mods3, mod_base, 2, layout)
        else:
            proj = dense(h, hgrn_w_in, li, F32)
            lb = lower_bounds[li]
            of_p, ob_p, st = hgrn_scan(proj, lb, 0, batch, seq, emit_state=True)
            of_s, ob_s = hgrn_scan(proj, lb, n_p, dec_batch, dec_seq, s0=state_hgrn[:, li])
            new_s.append(st)
            mix = jnp.zeros((n_p + n_s, proj.shape[1] // 5), BF16)
            mix = hgrn_post(of_p, ob_p, proj, 0, hgrn_norm_g[li], mix)
            mix = hgrn_post(of_s, ob_s, proj, n_p, hgrn_norm_g[li], mix)
            x = dense_residual(mix, hgrn_w_out, li, x, mods3, mod_base, 2, layout)
        x = moe(x, norm2_g[layer], mods3, mod_base, layer, moe_w_router[layer], moe_b_router[layer],
                moe_w_gu, moe_b_gu, moe_w_down, moe_b_down, layout)

    return (x[:n_p].reshape(batch, seq, d), x[n_p:].reshape(dec_batch, dec_seq, d),
            jnp.stack(new_ka, axis=1), jnp.stack(new_va, axis=1), jnp.stack(new_kb, axis=1),
            jnp.stack(new_vb, axis=1), jnp.stack(new_s, axis=1))
```

```python
import functools
import math

import jax
import jax.numpy as jnp
from jax import lax
from jax.experimental import pallas as pl
from jax.experimental.pallas import tpu as pltpu

F32 = jnp.float32
BF16 = jnp.bfloat16

HEAD_DIM = 128
A_HEADS = 16
A_KV_HEADS = 4
B_HEADS = 16
B_KV_HEADS = 4
WINDOW = 128
Q_BLOCK = 128
GRID_W = 64
ROPE_BASE = 10000.0
HGRN_HEADS = 32
HGRN_DK = 128
N_EXPERTS = 32
TOP_K = 4
SWIGLU_ALPHA = 1.702
SWIGLU_LIMIT = 7.0
N_MOD = 6
N_COND_ROWS = 8
EPS = 1e-6
LOG2_E = math.log2(math.e)
Q_PRESCALE = HEAD_DIM ** -0.5 * LOG2_E

VMEM_LIMIT = 48 * 1024 * 1024
VMEM_LIMIT_BIG = 56 * 1024 * 1024
HGRN_CHUNK = 128
MOE_BLOCK = 256
MOE_DOWN_TILE = 2048


def _params(sem, vmem=VMEM_LIMIT):
    return pltpu.CompilerParams(dimension_semantics=sem, vmem_limit_bytes=vmem)


def _cond_row(i, tm, n_prompt_tok, dec_seq):
    tok0 = i * tm
    return jnp.where(tok0 < n_prompt_tok, 0, 1 + (tok0 - n_prompt_tok) // dec_seq)


def _ada_kernel(c_ref, w_ref, b_ref, o_ref):
    c = c_ref[...]
    s = c * jax.nn.sigmoid(c)
    o_ref[...] = jnp.dot(s.astype(BF16), w_ref[...].astype(BF16), preferred_element_type=F32) + b_ref[...]


def ada_modulation(cond, ada_w, ada_b, tn=512):
    depth, d, n = ada_w.shape
    return pl.pallas_call(
        _ada_kernel,
        out_shape=jax.ShapeDtypeStruct((depth, N_COND_ROWS, n), F32),
        grid=(depth, n // tn),
        in_specs=[
            pl.BlockSpec((N_COND_ROWS, d), lambda l, j: (0, 0)),
            pl.BlockSpec((None, d, tn), lambda l, j: (l, 0, j)),
            pl.BlockSpec((None, 1, tn), lambda l, j: (l, 0, j)),
        ],
        out_specs=pl.BlockSpec((None, N_COND_ROWS, tn), lambda l, j: (l, 0, j)),
        compiler_params=_params(("arbitrary", "arbitrary")),
        name="ada_modulation",
    )(cond, ada_w, ada_b.reshape(depth, 1, n))


def _norm_mod(x, g, sc, sh):
    ms = jnp.mean(x * x, axis=-1, keepdims=True)
    y = x * lax.rsqrt(ms + EPS) * g
    return y * (1.0 + sc) + sh


def _norm_mod_kernel(x_ref, g_ref, sh_ref, sc_ref, o_ref):
    o_ref[...] = _norm_mod(x_ref[...], g_ref[...], sc_ref[0], sh_ref[0]).astype(o_ref.dtype)


def _mod_spec(d, mod_base, slot, tm, layout):
    return pl.BlockSpec((1, 1, d), lambda i: (mod_base + _cond_row(i, tm, *layout) * N_MOD + slot, 0, 0))


def norm_modulate(x, gain, mods3, mod_base, layout, tm=256):
    t, d = x.shape
    return pl.pallas_call(
        _norm_mod_kernel,
        out_shape=jax.ShapeDtypeStruct((t, d), BF16),
        grid=(t // tm,),
        in_specs=[
            pl.BlockSpec((tm, d), lambda i: (i, 0)),
            pl.BlockSpec((1, d), lambda i: (0, 0)),
            _mod_spec(d, mod_base, 0, tm, layout),
            _mod_spec(d, mod_base, 1, tm, layout),
        ],
        out_specs=pl.BlockSpec((tm, d), lambda i: (i, 0)),
        compiler_params=_params(("arbitrary",)),
        name="norm_modulate",
    )(x, gain.reshape(1, d), mods3, mods3)


def _norm_router_kernel(x_ref, g_ref, sh_ref, sc_ref, wr_ref, br_ref, hp_ref, idx_ref, gate_ref):
    h = _norm_mod(x_ref[...], g_ref[...], sc_ref[0], sh_ref[0])
    hp_ref[...] = _pack_halves(h)

    logits = jnp.dot(h, wr_ref[...], precision=lax.Precision.HIGHEST, preferred_element_type=F32) + br_ref[...]
    n_e = logits.shape[1]
    lane = lax.broadcasted_iota(jnp.int32, logits.shape, 1).astype(F32)
    vals = logits
    top_v, top_i = [], []
    for _ in range(TOP_K):
        m = jnp.max(vals, axis=-1, keepdims=True)
        first = jnp.min(jnp.where(vals == m, lane, float(n_e)), axis=-1, keepdims=True)
        vals = jnp.where(lane == first, -jnp.inf, vals)
        top_v.append(m)
        top_i.append(first)
    ex = [jnp.exp(v - top_v[0]) for v in top_v]
    denom = ex[0] + ex[1] + ex[2] + ex[3]
    for k in range(TOP_K):
        idx_ref[:, k:k + 1] = top_i[k].astype(jnp.int32)
        gate_ref[:, k:k + 1] = ex[k] / denom


def norm_modulate_route(x, gain, mods3, mod_base, w_router, b_router, layout, tm=256):
    t, d = x.shape
    n_e = w_router.shape[1]
    return pl.pallas_call(
        _norm_router_kernel,
        out_shape=(
            jax.ShapeDtypeStruct((t, d // 2), jnp.uint32),
            jax.ShapeDtypeStruct((t, TOP_K), jnp.int32),
            jax.ShapeDtypeStruct((t, TOP_K), F32),
        ),
        grid=(t // tm,),
        in_specs=[
            pl.BlockSpec((tm, d), lambda i: (i, 0)),
            pl.BlockSpec((1, d), lambda i: (0, 0)),
            _mod_spec(d, mod_base, 3, tm, layout),
            _mod_spec(d, mod_base, 4, tm, layout),
            pl.BlockSpec((d, n_e), lambda i: (0, 0)),
            pl.BlockSpec((1, n_e), lambda i: (0, 0)),
        ],
        out_specs=(
            pl.BlockSpec((tm, d // 2), lambda i: (i, 0)),
            pl.BlockSpec((tm, TOP_K), lambda i: (i, 0)),
            pl.BlockSpec((tm, TOP_K), lambda i: (i, 0)),
        ),
        compiler_params=_params(("arbitrary",)),
        name="norm_modulate_route",
    )(x, gain.reshape(1, d), mods3, mods3, w_router, b_router.reshape(1, n_e))


def _dense_kernel(a_ref, w_ref, o_ref, wbf_ref):
    @pl.when(pl.program_id(1) == 0)
    def _():
        wbf_ref[...] = w_ref[...].astype(BF16)

    o_ref[...] = jnp.dot(a_ref[...], wbf_ref[...], preferred_element_type=F32).astype(o_ref.dtype)


def _dense_residual_kernel(a_ref, w_ref, x_ref, gate_ref, o_ref, wbf_ref):
    @pl.when(pl.program_id(1) == 0)
    def _():
        wbf_ref[...] = w_ref[...].astype(BF16)

    o_ref[...] = x_ref[...] + gate_ref[0] * jnp.dot(a_ref[...], wbf_ref[...], preferred_element_type=F32)


def dense(a, w3, layer, out_dtype, tm=1024, tn=512):
    t, k = a.shape
    n = w3.shape[2]
    return pl.pallas_call(
        _dense_kernel,
        out_shape=jax.ShapeDtypeStruct((t, n), out_dtype),
        grid=(n // tn, t // tm),
        in_specs=[
            pl.BlockSpec((tm, k), lambda j, i: (i, 0)),
            pl.BlockSpec((None, k, tn), lambda j, i: (layer, 0, j)),
        ],
        out_specs=pl.BlockSpec((tm, tn), lambda j, i: (i, j)),
        scratch_shapes=[pltpu.VMEM((k, tn), BF16)],
        compiler_params=_params(("arbitrary", "arbitrary")),
        name="dense",
    )(a, w3)


def dense_residual(a, w3, layer, x, mods3, mod_base, slot, layout, tm=512, tn=512):
    t, k = a.shape
    n = w3.shape[2]
    return pl.pallas_call(
        _dense_residual_kernel,
        out_shape=jax.ShapeDtypeStruct((t, n), F32),
        grid=(n // tn, t // tm),
        in_specs=[
            pl.BlockSpec((tm, k), lambda j, i: (i, 0)),
            pl.BlockSpec((None, k, tn), lambda j, i: (layer, 0, j)),
            pl.BlockSpec((tm, tn), lambda j, i: (i, j)),
            pl.BlockSpec((1, 1, tn), lambda j, i: (mod_base + _cond_row(i, tm, *layout) * N_MOD + slot, 0, j)),
        ],
        out_specs=pl.BlockSpec((tm, tn), lambda j, i: (i, j)),
        scratch_shapes=[pltpu.VMEM((k, tn), BF16)],
        input_output_aliases={2: 0},
        compiler_params=_params(("arbitrary", "arbitrary")),
        name="dense_residual",
    )(a, w3, x, mods3)


def _head_rmsnorm(x, g):
    return x * lax.rsqrt(jnp.mean(x * x, axis=-1, keepdims=True) + EPS) * g


def _qkv_post_kernel(qkv_ref, cos_ref, sin_ref, qna_ref, kna_ref, qnb_ref, knb_ref, o_ref, kn_ref):
    cos = cos_ref[...]
    sin = sin_ref[...]
    even = (lax.broadcasted_iota(jnp.int32, cos.shape, 1) & 1) == 0
    segs = ((A_HEADS, qna_ref, False), (A_KV_HEADS, kna_ref, True), (A_KV_HEADS, None, False),
            (B_HEADS, qnb_ref, False), (B_KV_HEADS, knb_ref, True), (B_KV_HEADS, None, False))
    head = 0
    k_head = 0
    for n_heads, gain_ref, is_key in segs:
        for _ in range(n_heads):
            sl = slice(head * HEAD_DIM, (head + 1) * HEAD_DIM)
            x = qkv_ref[:, sl]
            if gain_ref is not None:
                x = _head_rmsnorm(x, gain_ref[...])
                if is_key:
                    kn_ref[:, k_head * HEAD_DIM:(k_head + 1) * HEAD_DIM] = x
                    k_head += 1
                partner = jnp.where(even, pltpu.roll(x, HEAD_DIM - 1, axis=1), pltpu.roll(x, 1, axis=1))
                x = x * cos + partner * sin
                if not is_key:
                    x = x * Q_PRESCALE
            o_ref[:, sl] = x.astype(o_ref.dtype)
            head += 1


def qkv_post(qkv, cos, sin, qn_a, kn_a, qn_b, kn_b, tm=256):
    t, n = qkv.shape
    n_k = (A_KV_HEADS + B_KV_HEADS) * HEAD_DIM
    gain = lambda g: g.reshape(1, HEAD_DIM)
    gspec = pl.BlockSpec((1, HEAD_DIM), lambda i: (0, 0))
    return pl.pallas_call(
        _qkv_post_kernel,
        out_shape=(jax.ShapeDtypeStruct((t, n), BF16), jax.ShapeDtypeStruct((t, n_k), F32)),
        grid=(t // tm,),
        in_specs=[
            pl.BlockSpec((tm, n), lambda i: (i, 0)),
            pl.BlockSpec((tm, HEAD_DIM), lambda i: (i, 0)),
            pl.BlockSpec((tm, HEAD_DIM), lambda i: (i, 0)),
            gspec, gspec, gspec, gspec,
        ],
        out_specs=(pl.BlockSpec((tm, n), lambda i: (i, 0)), pl.BlockSpec((tm, n_k), lambda i: (i, 0))),
        compiler_params=_params(("arbitrary",)),
        name="qkv_post",
    )(qkv, cos, sin, gain(qn_a), gain(kn_a), gain(qn_b), gain(kn_b))


def _attn_kernel(*refs, rep, has_cache, window, has_sink):
    it = iter(refs)
    q_ref, k_ref, v_ref = next(it), next(it), next(it)
    kc_ref, vc_ref = (next(it), next(it)) if has_cache else (None, None)
    sink_ref = next(it) if has_sink else None
    next(it)
    o_ref = next(it)

    qi = pl.program_id(2)
    nt = (((1,), (1,)), ((), ()))
    if window:
        span = Q_BLOCK + 2 * WINDOW
        n_local = k_ref.shape[0]
        start = pl.multiple_of(jnp.clip(qi * Q_BLOCK - WINDOW, 0, n_local - span), Q_BLOCK)
        k1 = k_ref[pl.ds(start, span), :]
        v1 = v_ref[pl.ds(start, span), :]
        qpos = qi * Q_BLOCK + lax.broadcasted_iota(jnp.int32, (Q_BLOCK, span), 0)
        kpos = start + lax.broadcasted_iota(jnp.int32, (Q_BLOCK, span), 1)
        in_band = jnp.abs(qpos - kpos) <= WINDOW
    else:
        k1 = k_ref[...]
        v1 = v_ref[...]
    if has_cache:
        kc = kc_ref[...]
        vc = vc_ref[...]
    for r in range(rep):
        qh = q_ref[:, r * HEAD_DIM:(r + 1) * HEAD_DIM]
        s1 = lax.dot_general(qh, k1, nt, preferred_element_type=F32)
        if window:
            s1 = jnp.where(in_band, s1, -jnp.inf)
        m = jnp.max(s1, axis=-1, keepdims=True)
        if has_cache:
            s2 = lax.dot_general(qh, kc, nt, preferred_element_type=F32)
            m = jnp.maximum(m, jnp.max(s2, axis=-1, keepdims=True))
        if has_sink:
            sink = sink_ref[r * Q_BLOCK:(r + 1) * Q_BLOCK, :] * LOG2_E
            m = jnp.maximum(m, sink)
        p1 = jnp.exp2(s1 - m)
        denom = jnp.sum(p1, axis=-1, keepdims=True)
        acc = jnp.dot(p1.astype(BF16), v1, preferred_element_type=F32)
        if has_cache:
            p2 = jnp.exp2(s2 - m)
            denom = denom + jnp.sum(p2, axis=-1, keepdims=True)
            acc = acc + jnp.dot(p2.astype(BF16), vc, preferred_element_type=F32)
        if has_sink:
            denom = denom + jnp.exp2(sink - m)
        o_ref[:, r * HEAD_DIM:(r + 1) * HEAD_DIM] = (acc / denom).astype(o_ref.dtype)


def attention(qkv, q_col, k_col, v_col, n_heads, n_kv, row0, n_seq, seq_len, out, out_col, cache_k=None,
              cache_v=None, sink=None, window=False):
    rep = n_heads // n_kv
    qw = rep * HEAD_DIM
    nqb = seq_len // Q_BLOCK
    rb0 = row0 // Q_BLOCK
    sb0 = row0 // seq_len
    in_specs = [
        pl.BlockSpec((Q_BLOCK, qw), lambda b, g, i: (rb0 + b * nqb + i, q_col // qw + g)),
        pl.BlockSpec((seq_len, HEAD_DIM), lambda b, g, i: (sb0 + b, k_col // HEAD_DIM + g)),
        pl.BlockSpec((seq_len, HEAD_DIM), lambda b, g, i: (sb0 + b, v_col // HEAD_DIM + g)),
    ]
    args = [qkv, qkv, qkv]
    if cache_k is not None:
        n_past = cache_k.shape[1]
        spec = pl.BlockSpec((None, n_past, HEAD_DIM), lambda b, g, i: (b, 0, g))
        in_specs += [spec, spec]
        args += [cache_k, cache_v]
    if sink is not None:
        sink_rows = jnp.repeat(sink.astype(F32), Q_BLOCK).reshape(n_kv * rep * Q_BLOCK, 1)
        in_specs.append(pl.BlockSpec((rep * Q_BLOCK, 1), lambda b, g, i: (g, 0)))
        args.append(sink_rows)
    aliases = {len(args): 0}
    in_specs.append(pl.BlockSpec(memory_space=pl.ANY))
    args.append(out)
    return pl.pallas_call(
        functools.partial(_attn_kernel, rep=rep, has_cache=cache_k is not None, window=window,
                          has_sink=sink is not None),
        out_shape=jax.ShapeDtypeStruct(out.shape, out.dtype),
        grid=(n_seq, n_kv, nqb),
        in_specs=in_specs,
        out_specs=pl.BlockSpec((Q_BLOCK, qw), lambda b, g, i: (rb0 + b * nqb + i, out_col // qw + g)),
        input_output_aliases=aliases,
        compiler_params=_params(("arbitrary", "arbitrary", "arbitrary")),
        name="attention",
    )(*args)


def _hgrn_chunk(q, z, v, lb, st, ups, pair_ref, forward):
    c = q.shape[0]
    e = jnp.exp(-jnp.abs(z))
    r = 1.0 / (1.0 + e)
    pos = z >= 0
    sig_pos = jnp.where(pos, r, e * r)
    sig_neg = jnp.where(pos, e * r, r)
    f = lb + (1.0 - lb) * sig_pos
    logf = jnp.log(f)
    k = (1.0 - lb) * sig_neg
    nt = (((1,), (1,)), ((), ()))

    a = logf if forward else jnp.zeros_like(logf)
    tot = logf
    att = None
    for lvl, up in enumerate(ups):
        half = 1 << lvl
        if half == 1:
            ex = jnp.where(up, f, 1.0) if forward else jnp.where(up, 1.0, f)
        else:
            ex = jnp.exp(jnp.where(up, a, tot - a))
        pairs = lax.dot_general((q * ex).astype(BF16), (k * ex).astype(BF16), nt, preferred_element_type=F32)
        pairs = pairs * pair_ref[lvl]
        att = pairs if att is None else att + pairs
        tot_dn = pltpu.roll(tot, half, axis=0)
        tot_up = pltpu.roll(tot, c - half, axis=0)
        a = a + jnp.where(up, tot_dn, 0.0)
        tot = tot + jnp.where(up, tot_dn, tot_up)

    q_arg, k_arg = (a, tot - a) if forward else (tot - a, a)
    vb = v.astype(BF16)
    o = lax.dot_general((q * jnp.exp(q_arg)).astype(BF16), st.astype(BF16), nt, preferred_element_type=F32)
    o = o + jnp.dot(att.astype(BF16), vb, preferred_element_type=F32)
    o = o + jnp.sum(q * k, axis=-1, keepdims=True) * v
    upd = lax.dot_general(vb, (k * jnp.exp(k_arg)).astype(BF16), (((0,), (0,)), ((), ())),
                          preferred_element_type=F32)
    st_new = jnp.exp(tot[0:1, :]) * st + upd
    return o, st_new


def _hgrn_scan_kernel(*refs, hpg, has_s0, emit_state):
    it = iter(refs)
    qf, zf, vf, qb, zb, vb, lbf, lbb = [next(it) for _ in range(8)]
    s0 = next(it) if has_s0 else None
    of, ob = next(it), next(it)
    sfin = next(it) if emit_state else None
    st, pair_ref = next(it), next(it)
    step = pl.program_id(2)
    c = qf.shape[0]
    levels = pair_ref.shape[1]

    @pl.when(step == 0)
    def _init():
        if has_s0:
            for d in range(2):
                for h in range(hpg):
                    st[d, h] = s0[0, d, h].T
        else:
            st[...] = jnp.zeros(st.shape, F32)
        rowc = lax.broadcasted_iota(jnp.int32, (c, c), 0)
        colc = lax.broadcasted_iota(jnp.int32, (c, c), 1)
        for lvl in range(levels):
            siblings = ((rowc ^ colc) >> lvl) == 1
            q_upper = (rowc & (1 << lvl)) != 0
            pair_ref[0, lvl] = jnp.where(jnp.logical_and(siblings, q_upper), 1.0, 0.0)
            pair_ref[1, lvl] = jnp.where(jnp.logical_and(siblings, jnp.logical_not(q_upper)), 1.0, 0.0)

    row = lax.broadcasted_iota(jnp.int32, (c, HGRN_DK), 0)
    ups = [(row & (1 << lvl)) != 0 for lvl in range(levels)]
    for h in range(hpg):
        sl = slice(h * HGRN_DK, (h + 1) * HGRN_DK)
        for d, (q_r, z_r, v_r, lb_r, o_r) in enumerate(((qf, zf, vf, lbf, of), (qb, zb, vb, lbb, ob))):
            o, st_new = _hgrn_chunk(q_r[:, sl], z_r[:, sl], v_r[:, sl], lb_r[:, sl], st[d, h], ups, pair_ref.at[d],
                                    forward=(d == 0))
            o_r[:, sl] = o
            st[d, h] = st_new

    if emit_state:
        @pl.when(step == pl.num_programs(2) - 1)
        def _fin():
            for d in range(2):
                for h in range(hpg):
                    sfin[0, d, h] = st[d, h].T


def hgrn_scan(proj, lb, row0, n_seq, seq_len, s0=None, emit_state=False, chunk=HGRN_CHUNK, hpg=4):
    hk = HGRN_HEADS * HGRN_DK
    hw = hpg * HGRN_DK
    n_hg = HGRN_HEADS // hpg
    nc = seq_len // chunk
    rb0 = row0 // chunk
    seg = hk // hw

    def fwd(col):
        return pl.BlockSpec((chunk, hw), lambda b, g, c: (rb0 + b * nc + c, col * seg + g))

    def bwd(col):
        return pl.BlockSpec((chunk, hw), lambda b, g, c: (rb0 + b * nc + nc - 1 - c, col * seg + g))

    in_specs = [fwd(0), fwd(1), fwd(3), bwd(0), bwd(2), bwd(3),
                pl.BlockSpec((1, hw), lambda b, g, c: (0, g)), pl.BlockSpec((1, hw), lambda b, g, c: (0, g))]
    args = [proj, proj, proj, proj, proj, proj, lb[0:1], lb[1:2]]
    state_spec = pl.BlockSpec((1, 2, hpg, HGRN_DK, HGRN_DK), lambda b, g, c: (b, 0, g, 0, 0))
    if s0 is not None:
        in_specs.append(state_spec)
        args.append(s0)
    n_rows = n_seq * seq_len
    out_shape = [jax.ShapeDtypeStruct((n_rows, hk), F32), jax.ShapeDtypeStruct((n_rows, hk), F32)]
    out_specs = [pl.BlockSpec((chunk, hw), lambda b, g, c: (b * nc + c, g)),
                 pl.BlockSpec((chunk, hw), lambda b, g, c: (b * nc + nc - 1 - c, g))]
    if emit_state:
        out_shape.append(jax.ShapeDtypeStruct((n_seq, 2, HGRN_HEADS, HGRN_DK, HGRN_DK), F32))
        out_specs.append(state_spec)
    return pl.pallas_call(
        functools.partial(_hgrn_scan_kernel, hpg=hpg, has_s0=s0 is not None, emit_state=emit_state),
        out_shape=tuple(out_shape),
        grid=(n_seq, n_hg, nc),
        in_specs=in_specs,
        out_specs=tuple(out_specs),
        scratch_shapes=[pltpu.VMEM((2, hpg, HGRN_DK, HGRN_DK), F32),
                        pltpu.VMEM((2, int(math.log2(chunk)), chunk, chunk), F32)],
        compiler_params=_params(("arbitrary", "arbitrary", "arbitrary")),
        name="hgrn_scan",
    )(*args)


def _hgrn_post_kernel(of_ref, ob_ref, g_ref, ng_ref, _, o_ref):
    for h in range(HGRN_HEADS):
        sl = slice(h * HGRN_DK, (h + 1) * HGRN_DK)
        o = of_ref[:, sl] + ob_ref[:, sl]
        g = g_ref[:, sl]
        y = _head_rmsnorm(o, ng_ref[:, sl]) * (g * jax.nn.sigmoid(g))
        o_ref[:, sl] = y.astype(o_ref.dtype)


def hgrn_post(o_f, o_b, proj, row0, norm_g, out, tm=256):
    t, hk = o_f.shape
    rb0 = row0 // tm
    spec = pl.BlockSpec((tm, hk), lambda i: (i, 0))
    return pl.pallas_call(
        _hgrn_post_kernel,
        out_shape=jax.ShapeDtypeStruct(out.shape, out.dtype),
        grid=(t // tm,),
        in_specs=[spec, spec, pl.BlockSpec((tm, hk), lambda i: (rb0 + i, 4)), pl.BlockSpec((1, hk), lambda i: (0, 0)),
                  pl.BlockSpec(memory_space=pl.ANY)],
        out_specs=pl.BlockSpec((tm, hk), lambda i: (rb0 + i, 0)),
        input_output_aliases={4: 0},
        compiler_params=_params(("arbitrary",)),
        name="hgrn_post",
    )(o_f, o_b, proj, norm_g.reshape(1, hk), out)


def moe_layout(idx, block):
    n_assign = idx.size
    flat_e = idx.reshape(-1)
    pos = jnp.arange(n_assign, dtype=jnp.int32)
    experts = jnp.arange(N_EXPERTS, dtype=jnp.int32)
    e_sorted, order = lax.sort((flat_e, pos), num_keys=1, is_stable=True)
    counts = jnp.sum((flat_e[:, None] == experts[None, :]).astype(jnp.int32), axis=0)
    padded = (counts + block - 1) // block * block
    pad_end = jnp.cumsum(padded)
    pad_start = pad_end - padded
    start = jnp.cumsum(counts) - counts
    n_blocks = -(-(n_assign + N_EXPERTS * (block - 1)) // block)
    first_slot = jnp.arange(n_blocks, dtype=jnp.int32) * block
    block_e = jnp.minimum(jnp.sum((pad_end[None, :] <= first_slot[:, None]).astype(jnp.int32), axis=1), N_EXPERTS - 1)
    block_off = first_slot - pad_start[block_e]
    block_cnt = jnp.clip(counts[block_e] - block_off, 0, block)
    block_src = start[block_e] + block_off
    n_used = (pad_end[-1:] // block).astype(jnp.int32)
    shift = jnp.sum(jnp.where(e_sorted[:, None] == experts[None, :], (pad_start - start)[None, :], 0), axis=1)
    _, slot_of = lax.sort((order, pos + shift), num_keys=1)
    return order, slot_of, block_e, block_src, block_cnt, n_used


def _pack_halves(x):
    half = x.shape[1] // 2
    bits = lax.bitcast_convert_type(x.astype(BF16).astype(F32), jnp.uint32)
    return (bits[:, half:] & jnp.uint32(0xFFFF0000)) | (bits[:, :half] >> 16)


def _unpack_halves(w):
    return lax.bitcast_convert_type(w << 16, F32), lax.bitcast_convert_type(w & jnp.uint32(0xFFFF0000), F32)


def _gather_kernel(order_ref, src_ref, cnt_ref, hp_hbm, o_hbm, buf, row_sem, out_sem):
    rows = buf.shape[1]
    i = pl.program_id(0)
    last_step = pl.num_programs(0) - 1
    last_assign = order_ref.shape[0] - 1

    def issue_rows(step, slot):
        src0 = src_ref[step]
        cnt = cnt_ref[step]

        def body(r, carry):
            assign = order_ref[jnp.minimum(src0 + r, last_assign)]
            tok = jnp.where(r < cnt, assign // TOP_K, 0)
            pltpu.make_async_copy(hp_hbm.at[tok], buf.at[slot, r], row_sem.at[slot]).start()
            return carry

        lax.fori_loop(0, rows, body, 0, unroll=8)

    def block_out(step, slot):
        return pltpu.make_async_copy(buf.at[slot], o_hbm.at[pl.ds(step * rows, rows)], out_sem.at[slot])

    slot = i % 2
    other = (i + 1) % 2

    @pl.when(i == 0)
    def _():
        issue_rows(0, 0)

    @pl.when(i < last_step)
    def _():
        @pl.when(i >= 1)
        def _():
            block_out(i - 1, other).wait()

        issue_rows(i + 1, other)

    pltpu.make_async_copy(hp_hbm.at[pl.ds(0, rows)], buf.at[slot], row_sem.at[slot]).wait()
    block_out(i, slot).start()

    @pl.when(i == last_step)
    def _():
        @pl.when(i >= 1)
        def _():
            block_out(i - 1, other).wait()

        block_out(i, slot).wait()


def moe_gather(order, block_src, block_cnt, hp, rows):
    n_slots = block_src.shape[0] * rows
    half = hp.shape[1]
    return pl.pallas_call(
        _gather_kernel,
        out_shape=jax.ShapeDtypeStruct((n_slots, half), hp.dtype),
        grid_spec=pltpu.PrefetchScalarGridSpec(
            num_scalar_prefetch=3,
            grid=(n_slots // rows,),
            in_specs=[pl.BlockSpec(memory_space=pl.ANY)],
            out_specs=pl.BlockSpec(memory_space=pl.ANY),
            scratch_shapes=[pltpu.VMEM((2, rows, half), hp.dtype), pltpu.SemaphoreType.DMA((2,)),
                            pltpu.SemaphoreType.DMA((2,))],
        ),
        compiler_params=_params(("arbitrary",)),
        name="moe_gather",
    )(order, block_src, block_cnt, hp)


def _first_block_of_expert(be_ref, b):
    return jnp.logical_or(b == 0, be_ref[b] != be_ref[jnp.maximum(b - 1, 0)])


def _used_block(b, nused_ref):
    return jnp.minimum(b, nused_ref[0] - 1)


def _moe_gu_kernel(be_ref, nused_ref, x_ref, wg_ref, wu_ref, bg_ref, bu_ref, o_ref, wg_bf, wu_bf):
    b = pl.program_id(1)

    @pl.when(b < nused_ref[0])
    def _():
        @pl.when(_first_block_of_expert(be_ref, b))
        def _():
            wg_bf[...] = wg_ref[...].astype(BF16)
            wu_bf[...] = wu_ref[...].astype(BF16)

        lo, hi = _unpack_halves(x_ref[...])
        lo = lo.astype(BF16)
        hi = hi.astype(BF16)
        half = lo.shape[1]

        def proj(w_bf, bias_ref):
            return (jnp.dot(lo, w_bf[:half, :], preferred_element_type=F32)
                    + jnp.dot(hi, w_bf[half:, :], preferred_element_type=F32) + bias_ref[...])

        gate = jnp.minimum(proj(wg_bf, bg_ref), SWIGLU_LIMIT)
        up = jnp.clip(proj(wu_bf, bu_ref), -SWIGLU_LIMIT, SWIGLU_LIMIT)
        o_ref[...] = (gate * jax.nn.sigmoid(SWIGLU_ALPHA * gate) * (up + 1.0)).astype(o_ref.dtype)

    @pl.when(b >= nused_ref[0])
    def _():
        o_ref[...] = jnp.zeros(o_ref.shape, o_ref.dtype)


def moe_gate_up(block_e, n_used, x_sorted, w_gu, b_gu, layer, block, tj=512):
    n_slots, half = x_sorted.shape
    d = 2 * half
    de = w_gu.shape[3] // 2
    n_j = de // tj
    b4 = b_gu.reshape(b_gu.shape[0], N_EXPERTS, 1, 2 * de)

    def w_spec(col0):
        return pl.BlockSpec((None, None, d, tj), lambda j, b, be, nu: (layer, be[_used_block(b, nu)], 0, col0 + j))

    def b_spec(col0):
        return pl.BlockSpec((None, None, 1, tj), lambda j, b, be, nu: (layer, be[_used_block(b, nu)], 0, col0 + j))

    return pl.pallas_call(
        _moe_gu_kernel,
        out_shape=jax.ShapeDtypeStruct((n_slots, de), BF16),
        grid_spec=pltpu.PrefetchScalarGridSpec(
            num_scalar_prefetch=2,
            grid=(n_j, n_slots // block),
            in_specs=[
                pl.BlockSpec((block, half), lambda j, b, be, nu: (_used_block(b, nu), 0)),
                w_spec(0), w_spec(n_j), b_spec(0), b_spec(n_j),
            ],
            out_specs=pl.BlockSpec((block, tj), lambda j, b, be, nu: (b, j)),
            scratch_shapes=[pltpu.VMEM((d, tj), BF16), pltpu.VMEM((d, tj), BF16)],
        ),
        compiler_params=_params(("arbitrary", "arbitrary"), VMEM_LIMIT_BIG),
        name="moe_gate_up",
    )(block_e, n_used, x_sorted, w_gu, w_gu, b4, b4)


def _moe_down_kernel(be_ref, nused_ref, a_ref, w_ref, b_ref, o_ref, w_bf):
    b = pl.program_id(1)

    @pl.when(b < nused_ref[0])
    def _():
        @pl.when(_first_block_of_expert(be_ref, b))
        def _():
            w_bf[...] = w_ref[...].astype(BF16)

        o_ref[...] = _pack_halves(jnp.dot(a_ref[...], w_bf[...], preferred_element_type=F32) + b_ref[...])

    @pl.when(b >= nused_ref[0])
    def _():
        o_ref[...] = jnp.zeros(o_ref.shape, o_ref.dtype)


def moe_down(block_e, n_used, act, w_down, b_down, layer, block, tn):
    n_slots, de = act.shape
    d = w_down.shape[3]
    b4 = b_down.reshape(b_down.shape[0], N_EXPERTS, 1, d)
    return pl.pallas_call(
        _moe_down_kernel,
        out_shape=jax.ShapeDtypeStruct((n_slots, d // 2), jnp.uint32),
        grid_spec=pltpu.PrefetchScalarGridSpec(
            num_scalar_prefetch=2,
            grid=(d // tn, n_slots // block),
            in_specs=[
                pl.BlockSpec((block, de), lambda n, b, be, nu: (_used_block(b, nu), 0)),
                pl.BlockSpec((None, None, de, tn), lambda n, b, be, nu: (layer, be[_used_block(b, nu)], 0, n)),
                pl.BlockSpec((None, None, 1, tn), lambda n, b, be, nu: (layer, be[_used_block(b, nu)], 0, n)),
            ],
            out_specs=pl.BlockSpec((block, tn // 2), lambda n, b, be, nu: (b, n)),
            scratch_shapes=[pltpu.VMEM((de, tn), BF16)],
        ),
        compiler_params=_params(("arbitrary", "arbitrary")),
        name="moe_down",
    )(block_e, n_used, act, w_down, b4)


def _combine_kernel(slot_ref, y_hbm, x_ref, gate_ref, g2_ref, o_ref, buf, sem, *, tn):
    tb, half_d = buf.shape[2], buf.shape[3]
    i = pl.program_id(0)

    def issue(step, slot):
        base = step * tb * TOP_K

        def body(t, carry):
            for k in range(TOP_K):
                pltpu.make_async_copy(y_hbm.at[slot_ref[base + t * TOP_K + k]], buf.at[slot, k, t],
                                      sem.at[slot]).start()
            return carry

        lax.fori_loop(0, tb, body, 0, unroll=2)

    @pl.when(i == 0)
    def _():
        issue(0, 0)

    @pl.when(i + 1 < pl.num_programs(0))
    def _():
        issue(i + 1, (i + 1) % 2)

    slot = i % 2
    for k in range(TOP_K):
        pltpu.make_async_copy(y_hbm.at[pl.ds(0, tb)], buf.at[slot, k], sem.at[slot]).wait()
    gate = gate_ref[...]
    hw = tn // 2
    for tile in range(2 * half_d // tn):
        acc_lo = acc_hi = None
        for k in range(TOP_K):
            lo, hi = _unpack_halves(buf[slot, k, :, tile * hw:(tile + 1) * hw])
            g = gate[:, k:k + 1]
            acc_lo = g * lo if acc_lo is None else acc_lo + g * lo
            acc_hi = g * hi if acc_hi is None else acc_hi + g * hi
        for acc, c0 in ((acc_lo, tile * tn), (acc_hi, tile * tn + hw)):
            o_ref[:, c0:c0 + hw] = x_ref[:, c0:c0 + hw] + g2_ref[0][:, c0:c0 + hw] * acc


def moe_combine(slot_of, y, x, gate, mods3, mod_base, layout, tn, tb=64):
    t, d = x.shape
    return pl.pallas_call(
        functools.partial(_combine_kernel, tn=tn),
        out_shape=jax.ShapeDtypeStruct((t, d), F32),
        grid_spec=pltpu.PrefetchScalarGridSpec(
            num_scalar_prefetch=1,
            grid=(t // tb,),
            in_specs=[
                pl.BlockSpec(memory_space=pl.ANY),
                pl.BlockSpec((tb, d), lambda i, s: (i, 0)),
                pl.BlockSpec((tb, TOP_K), lambda i, s: (i, 0)),
                pl.BlockSpec((1, 1, d), lambda i, s: (mod_base + _cond_row(i, tb, *layout) * N_MOD + 5, 0, 0)),
            ],
            out_specs=pl.BlockSpec((tb, d), lambda i, s: (i, 0)),
            scratch_shapes=[pltpu.VMEM((2, TOP_K, tb, d // 2), jnp.uint32), pltpu.SemaphoreType.DMA((2,))],
        ),
        input_output_aliases={2: 0},
        compiler_params=_params(("arbitrary",)),
        name="moe_combine",
    )(slot_of, y, x, gate, mods3)


def moe(x, gain, mods3, mod_base, layer, w_router, b_router, w_gu, b_gu, w_down, b_down, layout):
    hp, idx, gate = norm_modulate_route(x, gain, mods3, mod_base, w_router, b_router, layout)
    order, slot_of, block_e, block_src, block_cnt, n_used = moe_layout(idx, MOE_BLOCK)
    x_sorted = moe_gather(order, block_src, block_cnt, hp, MOE_BLOCK)
    act = moe_gate_up(block_e, n_used, x_sorted, w_gu, b_gu, layer, MOE_BLOCK)
    tn = min(MOE_DOWN_TILE, w_down.shape[3])
    y = moe_down(block_e, n_used, act, w_down, b_down, layer, MOE_BLOCK, tn)
    return moe_combine(slot_of, y, x, gate, mods3, mod_base, layout, tn)


def _rope_tables(n_prompt_tok, dec_batch, dec_seq):
    pos = jnp.arange(dec_seq)
    row = (pos // GRID_W).astype(F32)
    col = (pos % GRID_W).astype(F32)
    n_freq = HEAD_DIM // 4
    inv_freq = ROPE_BASE ** (-jnp.arange(n_freq, dtype=F32) / n_freq)
    ang = jnp.concatenate([row[:, None] * inv_freq, col[:, None] * inv_freq], axis=-1)
    cos = jnp.repeat(jnp.cos(ang), 2, axis=-1)
    sin = jnp.repeat(jnp.sin(ang), 2, axis=-1) * jnp.tile(jnp.array([-1.0, 1.0], F32), HEAD_DIM // 2)
    cos = jnp.concatenate([jnp.ones((n_prompt_tok, HEAD_DIM), F32), jnp.tile(cos, (dec_batch, 1))], axis=0)
    sin = jnp.concatenate([jnp.zeros((n_prompt_tok, HEAD_DIM), F32), jnp.tile(sin, (dec_batch, 1))], axis=0)
    return cos, sin


def kernel(x_prompt, x_sample, c, c_ctx, cache_k_a, cache_v_a, cache_k_b, cache_v_b, state_hgrn,
           ada_w, ada_b, norm1_g, norm2_g, attn_w_in, attn_qn_a, attn_kn_a, attn_qn_b, attn_kn_b,
           attn_sink_b, attn_w_out, hgrn_w_in, hgrn_lb_logits, hgrn_norm_g, hgrn_w_out,
           moe_w_router, moe_b_router, moe_w_gu, moe_b_gu, moe_w_down, moe_b_down):
    batch, seq, d = x_prompt.shape
    dec_batch, dec_seq, _ = x_sample.shape
    depth = ada_w.shape[0]
    n_p = batch * seq
    n_s = dec_batch * dec_seq
    layout = (n_p, dec_seq)
    assert 1 + dec_batch <= N_COND_ROWS

    cond = jnp.zeros((N_COND_ROWS, d), F32).at[0].set(c_ctx).at[1:1 + dec_batch].set(c)
    mods3 = ada_modulation(cond, ada_w, ada_b).reshape(depth * N_COND_ROWS * N_MOD, 1, d)
    x = jnp.concatenate([x_prompt.reshape(n_p, d), x_sample.reshape(n_s, d)], axis=0)
    cos, sin = _rope_tables(n_p, dec_batch, dec_seq)
    p_lb = jax.nn.softmax(hgrn_lb_logits.astype(F32), axis=0)
    lower_bounds = jnp.cumsum(p_lb, axis=0) - p_lb[0]

    qa_col = 0
    ka_col = qa_col + A_HEADS * HEAD_DIM
    va_col = ka_col + A_KV_HEADS * HEAD_DIM
    qb_col = va_col + A_KV_HEADS * HEAD_DIM
    kb_col = qb_col + B_HEADS * HEAD_DIM
    vb_col = kb_col + B_KV_HEADS * HEAD_DIM
    kv_a = A_KV_HEADS * HEAD_DIM
    kv_b = B_KV_HEADS * HEAD_DIM

    new_ka, new_va, new_kb, new_vb, new_s = [], [], [], [], []
    for layer in range(depth):
        mod_base = layer * N_COND_ROWS * N_MOD
        li = layer // 2
        h = norm_modulate(x, norm1_g[layer], mods3, mod_base, layout)
        if layer % 2 == 0:
            qkv = dense(h, attn_w_in, li, F32)
            qkv_bf, kn = qkv_post(qkv, cos, sin, attn_qn_a[li], attn_kn_a[li], attn_qn_b[li], attn_kn_b[li])
            new_ka.append(kn[:n_p, :kv_a].reshape(batch, seq, A_KV_HEADS, HEAD_DIM))
            new_kb.append(kn[:n_p, kv_a:].reshape(batch, seq, B_KV_HEADS, HEAD_DIM))
            new_va.append(qkv[:n_p, va_col:va_col + kv_a].reshape(batch, seq, A_KV_HEADS, HEAD_DIM))
            new_vb.append(qkv[:n_p, vb_col:vb_col + kv_b].reshape(batch, seq, B_KV_HEADS, HEAD_DIM))
            n_past = cache_k_a.shape[2]
            cache = lambda t, nkv: t[:, li].reshape(dec_batch, n_past, nkv * HEAD_DIM).astype(BF16)
            sink = attn_sink_b[li]
            b_col = A_HEADS * HEAD_DIM
            mix = jnp.zeros((n_p + n_s, (A_HEADS + B_HEADS) * HEAD_DIM), BF16)
            mix = attention(qkv_bf, qa_col, ka_col, va_col, A_HEADS, A_KV_HEADS, 0, batch, seq, mix, 0)
            mix = attention(qkv_bf, qb_col, kb_col, vb_col, B_HEADS, B_KV_HEADS, 0, batch, seq, mix, b_col, sink=sink)
            mix = attention(qkv_bf, qa_col, ka_col, va_col, A_HEADS, A_KV_HEADS, n_p, dec_batch, dec_seq, mix, 0,
                            cache_k=cache(cache_k_a, A_KV_HEADS), cache_v=cache(cache_v_a, A_KV_HEADS))
            mix = attention(qkv_bf, qb_col, kb_col, vb_col, B_HEADS, B_KV_HEADS, n_p, dec_batch, dec_seq, mix, b_col,
                            cache_k=cache(cache_k_b, B_KV_HEADS), cache_v=cache(cache_v_b, B_KV_HEADS),
                            sink=sink, window=True)
            x = dense_residual(mix, attn_w_out, li, x, mods3, mod_base, 2, layout)
        else:
            proj = dense(h, hgrn_w_in, li, F32)
            lb = lower_bounds[li]
            of_p, ob_p, st = hgrn_scan(proj, lb, 0, batch, seq, emit_state=True)
            of_s, ob_s = hgrn_scan(proj, lb, n_p, dec_batch, dec_seq, s0=state_hgrn[:, li])
            new_s.append(st)
            mix = jnp.zeros((n_p + n_s, proj.shape[1] // 5), BF16)
            mix = hgrn_post(of_p, ob_p, proj, 0, hgrn_norm_g[li], mix)
            mix = hgrn_post(of_s, ob_s, proj, n_p, hgrn_norm_g[li], mix)
            x = dense_residual(mix, hgrn_w_out, li, x, mods3, mod_base, 2, layout)
        x = moe(x, norm2_g[layer], mods3, mod_base, layer, moe_w_router[layer], moe_b_router[layer],
                moe_w_gu, moe_b_gu, moe_w_down, moe_b_down, layout)

    return (x[:n_p].reshape(batch, seq, d), x[n_p:].reshape(dec_batch, dec_seq, d),
            jnp.stack(new_ka, axis=1), jnp.stack(new_va, axis=1), jnp.stack(new_kb, axis=1),
            jnp.stack(new_vb, axis=1), jnp.stack(new_s, axis=1))
```
